```python
import math
import jax
import jax.numpy as jnp
from jax import lax
import numpy as np

D_MODEL = 1024
BATCH = 16
SEQ = 2048
DEPTH = 4

N_MIXERS = 3
HEAD_DIM = 64
N_SCORE_HEADS = 16
DIFF_HEADS = D_MODEL // (2 * HEAD_DIM)
DIFF_QK_WIDTH = DIFF_HEADS * 2 * HEAD_DIM
DIFF_V_DIM = 2 * HEAD_DIM
MOBA_HEADS = D_MODEL // HEAD_DIM
MOBA_BLOCK = 256
MOBA_TOPK = 3
NSA_HEADS = D_MODEL // HEAD_DIM
NSA_KV_GROUPS = 4
NSA_GROUP_SIZE = NSA_HEADS // NSA_KV_GROUPS
NSA_CMP_BLOCK = 32
NSA_CMP_STRIDE = 16
NSA_CMP_HIDDEN = 256
NSA_SEL_BLOCK = 64
NSA_SEL_TOPK = 16
NSA_WINDOW = 512
NSA_KV_WIDTH = NSA_KV_GROUPS * HEAD_DIM
NSA_IN_WIDTH = NSA_HEADS * HEAD_DIM + 6 * NSA_KV_WIDTH + 3 * NSA_HEADS
REL_BUCKETS = 32
REL_MAX_DIST = 1024
FFN_DIM = 3584
N_EXPERTS = 8
TOP_K_EXPERTS = 2
QUERY_BLOCK = 128
SPARSE_QUERY_BLOCK = 64
NORM_EPS = 1e-6
NEG_INF = -1e30
FORCE_SCORE = 1e30
ATTN_SCALE = HEAD_DIM ** -0.5

kernel_name = 'hybrid_diff_moba_nsa_moe_trunk'


def rms_norm(x, gain):
    xf = x.astype(jnp.float32)
    y = xf * lax.rsqrt(jnp.mean(xf * xf, axis=-1, keepdims=True) + NORM_EPS)
    return (y * gain.astype(jnp.float32)).astype(x.dtype)


def masked_softmax(logits, mask):
    logits = jnp.where(mask, logits, NEG_INF)
    m = jnp.max(logits, axis=-1, keepdims=True)
    p = jnp.where(mask, jnp.exp(logits - m), 0.0)
    return p / jnp.maximum(jnp.sum(p, axis=-1, keepdims=True), 1e-30)


def rel_bucket(dist):
    n = jnp.maximum(dist, 0)
    max_exact = REL_BUCKETS // 2
    nf = jnp.maximum(n, max_exact).astype(jnp.float32)
    large = max_exact + (jnp.log(nf / max_exact) / math.log(REL_MAX_DIST / max_exact)
                         * (REL_BUCKETS - max_exact)).astype(jnp.int32)
    return jnp.where(n < max_exact, n, jnp.minimum(large, REL_BUCKETS - 1))


def diff_lambda_init(layer):
    return 0.8 - 0.6 * math.exp(-0.3 * layer)


def differential_attention(h, w_in, w_out, q_gain, k_gain, lam_q1, lam_k1, lam_q2, lam_k2,
                           subln_gain, rel_bias, lambda_init):
    B, S, _ = h.shape
    q, k, v = jnp.split(h @ w_in, [DIFF_QK_WIDTH, 2 * DIFF_QK_WIDTH], axis=-1)
    q = rms_norm(q.reshape(B, S, DIFF_HEADS, 2, HEAD_DIM), q_gain)
    k = rms_norm(k.reshape(B, S, DIFF_HEADS, 2, HEAD_DIM), k_gain)
    v = v.reshape(B, S, DIFF_HEADS, DIFF_V_DIM)
    lam = (jnp.exp(jnp.sum((lam_q1 * lam_k1).astype(jnp.float32)))
           - jnp.exp(jnp.sum((lam_q2 * lam_k2).astype(jnp.float32))) + lambda_init)
    n_qb = S // QUERY_BLOCK
    q_blocks = q.reshape(B, n_qb, QUERY_BLOCK, DIFF_HEADS, 2, HEAD_DIM).swapaxes(0, 1)
    key_pos = jnp.arange(S)

    def block(args):
        qb, start = args
        dist = (start + jnp.arange(QUERY_BLOCK))[:, None] - key_pos[None, :]
        bias = rel_bias[rel_bucket(dist)].astype(jnp.float32)
        bias = bias.reshape(QUERY_BLOCK, S, DIFF_HEADS, 2).transpose(2, 3, 0, 1)
        logits = jnp.einsum('bqhmd,bkhmd->bhmqk', qb, k).astype(jnp.float32) * ATTN_SCALE + bias
        p = masked_softmax(logits, dist >= 0)
        a = p[:, :, 0] - lam * p[:, :, 1]
        return jnp.einsum('bhqk,bkhe->bqhe', a, v)

    o = lax.map(block, (q_blocks, jnp.arange(n_qb) * QUERY_BLOCK))
    o = o.swapaxes(0, 1).reshape(B, S, DIFF_HEADS, DIFF_V_DIM)
    o = rms_norm(o, subln_gain) * (1.0 - lambda_init)
    return o.reshape(B, S, DIFF_HEADS * DIFF_V_DIM).astype(h.dtype) @ w_out


def moba_attention(h, w_in, w_out, q_gain, k_gain, rel_bias):
    B, S, _ = h.shape
    H, d = MOBA_HEADS, HEAD_DIM
    q, k, v = jnp.split(h @ w_in, 3, axis=-1)
    q = rms_norm(q.reshape(B, S, H, d), q_gain).transpose(0, 2, 1, 3)
    k = rms_norm(k.reshape(B, S, H, d), k_gain).transpose(0, 2, 1, 3)
    v = v.reshape(B, S, H, d).transpose(0, 2, 1, 3)
    n_blk = -(-S // MOBA_BLOCK)
    pad = ((0, 0), (0, 0), (0, n_blk * MOBA_BLOCK - S), (0, 0))
    k = jnp.pad(k, pad)
    v = jnp.pad(v, pad)
    k_mean = jnp.mean(k.reshape(B, H, n_blk, MOBA_BLOCK, d).astype(jnp.float32), axis=3)
    topk = min(MOBA_TOPK, n_blk)
    Qc = SPARSE_QUERY_BLOCK
    n_sel_keys = topk * MOBA_BLOCK
    table = rel_bias.T
    h_ix = jnp.arange(H)[:, None, None]

    def per_seq(args):
        qs, ks, vs, km = args
        k_blk = ks.reshape(H, n_blk, MOBA_BLOCK, d)
        v_blk = vs.reshape(H, n_blk, MOBA_BLOCK, d)

        def chunk(start):
            qc = lax.dynamic_slice_in_dim(qs, start, Qc, axis=1)
            q_pos = start + jnp.arange(Qc)
            cur = start // MOBA_BLOCK
            gate = jnp.einsum('hqd,hnd->hqn', qc.astype(jnp.float32), km)
            gate = jnp.where(jnp.arange(n_blk) < cur, gate, NEG_INF)
            _, idx = lax.top_k(gate, topk)
            valid = jnp.repeat(idx < cur, MOBA_BLOCK, axis=-1)
            k_sel = k_blk[h_ix, idx].reshape(H, Qc, n_sel_keys, d)
            v_sel = v_blk[h_ix, idx].reshape(H, Qc, n_sel_keys, d)
            sel_pos = (idx[..., None] * MOBA_BLOCK + jnp.arange(MOBA_BLOCK)).reshape(H, Qc, n_sel_keys)
            sel_bias = table[h_ix, rel_bucket(q_pos[None, :, None] - sel_pos)]
            sel_logits = jnp.einsum('hqd,hqkd->hqk', qc, k_sel).astype(jnp.float32) * ATTN_SCALE + sel_bias
            own_start = cur * MOBA_BLOCK
            k_own = lax.dynamic_slice_in_dim(ks, own_start, MOBA_BLOCK, axis=1)
            v_own = lax.dynamic_slice_in_dim(vs, own_start, MOBA_BLOCK, axis=1)
            own_dist = q_pos[:, None] - (own_start + jnp.arange(MOBA_BLOCK))[None, :]
            own_logits = (jnp.einsum('hqd,hkd->hqk', qc, k_own).astype(jnp.float32) * ATTN_SCALE
                          + table[:, rel_bucket(own_dist)])
            logits = jnp.concatenate([sel_logits, own_logits], axis=-1)
            mask = jnp.concatenate([valid, jnp.broadcast_to(own_dist >= 0, (H, Qc, MOBA_BLOCK))], axis=-1)
            p = masked_softmax(logits, mask)
            return (jnp.einsum('hqk,hqkd->hqd', p[..., :n_sel_keys], v_sel)
                    + jnp.einsum('hqk,hkd->hqd', p[..., n_sel_keys:], v_own))

        o = lax.map(chunk, jnp.arange(S // Qc) * Qc)
        return o.transpose(1, 0, 2, 3).reshape(H, S, d)

    o = lax.map(per_seq, (q, k, v, k_mean))
    return o.transpose(0, 2, 1, 3).reshape(B, S, H * d).astype(h.dtype) @ w_out


def nsa_attention(h, w_in, w_out, q_gain, k_gain, cmp_pos_k, cmp_pos_v, cmp_k_w1, cmp_k_w2,
                  cmp_v_w1, cmp_v_w2, rel_bias):
    B, S, _ = h.shape
    G, R, d = NSA_KV_GROUPS, NSA_GROUP_SIZE, HEAD_DIM
    splits = np.cumsum([NSA_HEADS * d] + [NSA_KV_WIDTH] * 6).tolist()
    q, kc, vc, ks, vs, kw, vw, g = jnp.split(h @ w_in, splits, axis=-1)
    q = rms_norm(q.reshape(B, S, G, R, d), q_gain)
    gates = jax.nn.sigmoid(g.astype(jnp.float32)).reshape(B, S, G, R, 3)

    n_cmp = (S - NSA_CMP_BLOCK) // NSA_CMP_STRIDE + 1
    cmp_start = jnp.arange(n_cmp) * NSA_CMP_STRIDE
    win_idx = cmp_start[:, None] + jnp.arange(NSA_CMP_BLOCK)[None, :]

    def compress(x, pos, w1, w2):
        xw = x.reshape(B, S, G, d)[:, win_idx] + pos[:, None, :]
        xw = xw.transpose(0, 1, 3, 2, 4).reshape(B, n_cmp, G, NSA_CMP_BLOCK * d)
        return jax.nn.gelu(xw @ w1) @ w2

    k_cmp = rms_norm(compress(kc, cmp_pos_k, cmp_k_w1, cmp_k_w2), k_gain[0])
    v_cmp = compress(vc, cmp_pos_v, cmp_v_w1, cmp_v_w2)
    k_slc = rms_norm(ks.reshape(B, S, G, d), k_gain[1])
    v_slc = vs.reshape(B, S, G, d)
    pad_w = ((0, 0), (NSA_WINDOW, 0), (0, 0), (0, 0))
    k_win = jnp.pad(rms_norm(kw.reshape(B, S, G, d), k_gain[2]), pad_w)
    v_win = jnp.pad(vw.reshape(B, S, G, d), pad_w)

    n_sel = S // NSA_SEL_BLOCK
    topk = min(NSA_SEL_TOPK, n_sel)
    n_sel_keys = topk * NSA_SEL_BLOCK
    sel_start = jnp.arange(n_sel) * NSA_SEL_BLOCK
    overlap = jnp.clip(jnp.minimum(cmp_start[:, None] + NSA_CMP_BLOCK, sel_start[None, :] + NSA_SEL_BLOCK)
                       - jnp.maximum(cmp_start[:, None], sel_start[None, :]), 0).astype(jnp.float32) / NSA_CMP_STRIDE
    cmp_last = cmp_start + NSA_CMP_BLOCK - 1
    table = rel_bias.T.reshape(G, R, REL_BUCKETS)
    g_ix = jnp.arange(G)
    r_ix = jnp.arange(R)
    Qc = SPARSE_QUERY_BLOCK
    win_len = NSA_WINDOW + Qc

    def per_seq(args):
        qs, kcs, vcs, kss, vss, kws, vws = args
        k_blk = kss.reshape(n_sel, NSA_SEL_BLOCK, G, d).transpose(2, 0, 1, 3)
        v_blk = vss.reshape(n_sel, NSA_SEL_BLOCK, G, d).transpose(2, 0, 1, 3)

        def chunk(start):
            qc = lax.dynamic_slice_in_dim(qs, start, Qc, axis=0)
            q_pos = start + jnp.arange(Qc)
            c_logits = jnp.einsum('qgrd,ngd->grqn', qc, kcs).astype(jnp.float32) * ATTN_SCALE
            p_cmp = masked_softmax(c_logits, cmp_last[None, :] <= q_pos[:, None])
            o_cmp = jnp.einsum('grqn,ngd->qgrd', p_cmp, vcs)
            blk = jnp.arange(n_sel)
            cur = (q_pos // NSA_SEL_BLOCK)[:, None]
            imp = jnp.einsum('gqn,ns->gqs', jnp.sum(p_cmp, axis=1), overlap)
            forced = (blk == 0) | (blk == cur) | (blk == cur - 1)
            imp = jnp.where(forced, FORCE_SCORE, jnp.where(blk <= cur, imp, NEG_INF))
            _, idx = lax.top_k(imp, topk)
            k_sel = k_blk[g_ix[:, None, None], idx].reshape(G, Qc, n_sel_keys, d)
            v_sel = v_blk[g_ix[:, None, None], idx].reshape(G, Qc, n_sel_keys, d)
            sel_pos = (idx[..., None] * NSA_SEL_BLOCK + jnp.arange(NSA_SEL_BLOCK)).reshape(G, Qc, n_sel_keys)
            sel_dist = q_pos[None, :, None] - sel_pos
            sel_bias = table[g_ix[:, None, None, None], r_ix[None, :, None, None], rel_bucket(sel_dist)[:, None]]
            s_logits = jnp.einsum('qgrd,gqkd->grqk', qc, k_sel).astype(jnp.float32) * ATTN_SCALE + sel_bias
            p_sel = masked_softmax(s_logits, (sel_dist >= 0)[:, None])
            o_sel = jnp.einsum('grqk,gqkd->qgrd', p_sel, v_sel)
            k_w = lax.dynamic_slice_in_dim(kws, start, win_len, axis=0)
            v_w = lax.dynamic_slice_in_dim(vws, start, win_len, axis=0)
            w_pos = start - NSA_WINDOW + jnp.arange(win_len)
            w_dist = q_pos[:, None] - w_pos[None, :]
            w_mask = (w_dist >= 0) & (w_dist < NSA_WINDOW) & (w_pos >= 0)[None, :]
            w_logits = (jnp.einsum('qgrd,kgd->grqk', qc, k_w).astype(jnp.float32) * ATTN_SCALE
                        + table[:, :, rel_bucket(w_dist)])
            p_w = masked_softmax(w_logits, w_mask)
            o_w = jnp.einsum('grqk,kgd->qgrd', p_w, v_w)
            return jnp.stack([o_cmp, o_sel, o_w], axis=-1)

        out = lax.map(chunk, jnp.arange(S // Qc) * Qc)
        return out.reshape(S, G, R, d, 3)

    out = lax.map(per_seq, (q, k_cmp, v_cmp, k_slc, v_slc, k_win, v_win))
    o = jnp.sum(out * gates[..., None, :], axis=-1)
    return o.reshape(B, S, NSA_HEADS * d).astype(h.dtype) @ w_out


def swiglu(h, w_gate, w_up, w_down):
    return (jax.nn.silu(h @ w_gate) * (h @ w_up)) @ w_down


def moe_swiglu(h, w_router, b_router, we_gate, we_up, we_down):
    B, S, D = h.shape
    t = h.reshape(B * S, D)
    logits = (t @ w_router).astype(jnp.float32) + b_router.astype(jnp.float32)
    top_val, top_idx = lax.top_k(logits, TOP_K_EXPERTS)
    weights = jax.nn.softmax(top_val, axis=-1)
    gate = jnp.sum(jax.nn.one_hot(top_idx, N_EXPERTS, dtype=jnp.float32) * weights[..., None], axis=1)
    out = jnp.zeros_like(t)
    for e in range(N_EXPERTS):
        out = out + gate[:, e:e + 1].astype(t.dtype) * swiglu(t, we_gate[e], we_up[e], we_down[e])
    return out.reshape(B, S, D)


def setup_inputs(seed: int = 0) -> dict:
    keys = iter(jax.random.split(jax.random.key(seed), 96))
    D = D_MODEL

    def normal(shape, scale):
        return jax.random.normal(next(keys), shape, jnp.float32) * scale

    def gain(shape):
        return 1.0 + normal(shape, 0.02)

    inp = {'x': normal((BATCH, SEQ, D), 1.0),
           'rel_bias': normal((REL_BUCKETS, N_SCORE_HEADS), 0.3)}

    def add_diff(p):
        inp[p + 'w_in'] = normal((D, 2 * DIFF_QK_WIDTH + DIFF_HEADS * DIFF_V_DIM), D ** -0.5)
        inp[p + 'w_out'] = normal((DIFF_HEADS * DIFF_V_DIM, D), (DIFF_HEADS * DIFF_V_DIM) ** -0.5)
        inp[p + 'q_gain'] = gain((HEAD_DIM,))
        inp[p + 'k_gain'] = gain((HEAD_DIM,))
        for n in ('lam_q1', 'lam_k1', 'lam_q2', 'lam_k2'):
            inp[p + n] = normal((HEAD_DIM,), 0.1)
        inp[p + 'subln_gain'] = gain((DIFF_V_DIM,))

    def add_moba(p):
        inp[p + 'w_in'] = normal((D, 3 * MOBA_HEADS * HEAD_DIM), D ** -0.5)
        inp[p + 'w_out'] = normal((MOBA_HEADS * HEAD_DIM, D), (MOBA_HEADS * HEAD_DIM) ** -0.5)
        inp[p + 'q_gain'] = gain((HEAD_DIM,))
        inp[p + 'k_gain'] = gain((HEAD_DIM,))

    def add_nsa(p):
        inp[p + 'w_in'] = normal((D, NSA_IN_WIDTH), D ** -0.5)
        inp[p + 'w_out'] = normal((NSA_HEADS * HEAD_DIM, D), (NSA_HEADS * HEAD_DIM) ** -0.5)
        inp[p + 'q_gain'] = gain((HEAD_DIM,))
        inp[p + 'k_gain'] = gain((3, HEAD_DIM))
        inp[p + 'cmp_pos_k'] = normal((NSA_CMP_BLOCK, HEAD_DIM), 0.1)
        inp[p + 'cmp_pos_v'] = normal((NSA_CMP_BLOCK, HEAD_DIM), 0.1)
        for n in ('k', 'v'):
            inp[p + 'cmp_' + n + '_w1'] = normal((NSA_CMP_BLOCK * HEAD_DIM, NSA_CMP_HIDDEN), (NSA_CMP_BLOCK * HEAD_DIM) ** -0.5)
            inp[p + 'cmp_' + n + '_w2'] = normal((NSA_CMP_HIDDEN, HEAD_DIM), NSA_CMP_HIDDEN ** -0.5)

    def add_dense(p):
        inp[p + 'w_gate'] = normal((D, FFN_DIM), D ** -0.5)
        inp[p + 'w_up'] = normal((D, FFN_DIM), D ** -0.5)
        inp[p + 'w_down'] = normal((FFN_DIM, D), FFN_DIM ** -0.5)

    def add_moe(p):
        inp[p + 'w_router'] = normal((D, N_EXPERTS), D ** -0.5)
        inp[p + 'b_router'] = normal((N_EXPERTS,), 0.01)
        inp[p + 'we_gate'] = normal((N_EXPERTS, D, FFN_DIM), D ** -0.5)
        inp[p + 'we_up'] = normal((N_EXPERTS, D, FFN_DIM), D ** -0.5)
        inp[p + 'we_down'] = normal((N_EXPERTS, FFN_DIM, D), FFN_DIM ** -0.5)

    mixer_builders = (add_diff, add_moba, add_nsa)
    for i in range(DEPTH):
        p = f'l{i}_'
        inp[p + 'norm_mix'] = gain((D,))
        mixer_builders[i % N_MIXERS](p)
        inp[p + 'norm_ffn'] = gain((D,))
        (add_dense if i % 2 == 0 else add_moe)(p)
    return inp


def reference(x, rel_bias,
              l0_norm_mix, l0_w_in, l0_w_out, l0_q_gain, l0_k_gain, l0_lam_q1, l0_lam_k1, l0_lam_q2, l0_lam_k2,
              l0_subln_gain, l0_norm_ffn, l0_w_gate, l0_w_up, l0_w_down,
              l1_norm_mix, l1_w_in, l1_w_out, l1_q_gain, l1_k_gain, l1_norm_ffn, l1_w_router, l1_b_router,
              l1_we_gate, l1_we_up, l1_we_down,
              l2_norm_mix, l2_w_in, l2_w_out, l2_q_gain, l2_k_gain, l2_cmp_pos_k, l2_cmp_pos_v, l2_cmp_k_w1,
              l2_cmp_k_w2, l2_cmp_v_w1, l2_cmp_v_w2, l2_norm_ffn, l2_w_gate, l2_w_up, l2_w_down,
              l3_norm_mix, l3_w_in, l3_w_out, l3_q_gain, l3_k_gain, l3_lam_q1, l3_lam_k1, l3_lam_q2, l3_lam_k2,
              l3_subln_gain, l3_norm_ffn, l3_w_router, l3_b_router, l3_we_gate, l3_we_up, l3_we_down):
    layers = [
        (l0_norm_mix, (l0_w_in, l0_w_out, l0_q_gain, l0_k_gain, l0_lam_q1, l0_lam_k1, l0_lam_q2, l0_lam_k2, l0_subln_gain),
         l0_norm_ffn, (l0_w_gate, l0_w_up, l0_w_down)),
        (l1_norm_mix, (l1_w_in, l1_w_out, l1_q_gain, l1_k_gain),
         l1_norm_ffn, (l1_w_router, l1_b_router, l1_we_gate, l1_we_up, l1_we_down)),
        (l2_norm_mix, (l2_w_in, l2_w_out, l2_q_gain, l2_k_gain, l2_cmp_pos_k, l2_cmp_pos_v,
                       l2_cmp_k_w1, l2_cmp_k_w2, l2_cmp_v_w1, l2_cmp_v_w2),
         l2_norm_ffn, (l2_w_gate, l2_w_up, l2_w_down)),
        (l3_norm_mix, (l3_w_in, l3_w_out, l3_q_gain, l3_k_gain, l3_lam_q1, l3_lam_k1, l3_lam_q2, l3_lam_k2, l3_subln_gain),
         l3_norm_ffn, (l3_w_router, l3_b_router, l3_we_gate, l3_we_up, l3_we_down)),
    ]
    h = x
    for i in range(DEPTH):
        norm_mix, mix_p, norm_ffn, ffn_p = layers[i]
        hn = rms_norm(h, norm_mix)
        kind = i % N_MIXERS
        if kind == 0:
            mixed = differential_attention(hn, *mix_p, rel_bias, diff_lambda_init(i))
        elif kind == 1:
            mixed = moba_attention(hn, *mix_p, rel_bias)
        else:
            mixed = nsa_attention(hn, *mix_p, rel_bias)
        h = h + mixed.astype(h.dtype)
        hn = rms_norm(h, norm_ffn)
        if i % 2 == 0:
            f = swiglu(hn, *ffn_p)
        else:
            f = moe_swiglu(hn, *ffn_p)
        h = h + f.astype(h.dtype)
    return h
```

```python
import functools
import math

import jax
import jax.numpy as jnp
from jax import lax
from jax.experimental import pallas as pl
from jax.experimental.pallas import tpu as pltpu

F32 = jnp.float32
BF16 = jnp.bfloat16

D_MODEL = 1024
HEAD_DIM = 64
N_SCORE_HEADS = 16
DIFF_HEADS = 8
MOBA_BLOCK = 256
MOBA_TOPK = 3
NSA_KV_GROUPS = 4
NSA_GROUP_SIZE = 4
NSA_CMP_BLOCK = 32
NSA_CMP_STRIDE = 16
NSA_CMP_HIDDEN = 256
NSA_SEL_BLOCK = 64
NSA_SEL_TOPK = 16
NSA_WINDOW = 512
REL_BUCKETS = 32
REL_MAX_DIST = 1024
FFN_DIM = 3584
N_EXPERTS = 8
NORM_EPS = 1e-6
NEG_INF = -1e30
FORCE_SCORE = 1e30
ATTN_SCALE = HEAD_DIM ** -0.5

LANES = 128
ATT_TILE = 256
VMEM_LIMIT = 56 * 1024 * 1024


def _cparams(sem):
    return pltpu.CompilerParams(dimension_semantics=sem, vmem_limit_bytes=VMEM_LIMIT)


def _dot_nt(a, b):
    return lax.dot_general(a, b, (((1,), (1,)), ((), ())), preferred_element_type=F32)


def _norm_proj_kernel(x_ref, g_ref, w_ref, hg_ref, o_ref, xn_ref, *, n_norm_tiles, tn):
    j = pl.program_id(1)

    @pl.when(j == 0)
    def _():
        x = x_ref[...]
        r = lax.rsqrt(jnp.mean(x * x, axis=-1, keepdims=True) + NORM_EPS)
        xn_ref[...] = (x * r * g_ref[...]).astype(BF16)

    acc = jnp.dot(xn_ref[...], w_ref[...], preferred_element_type=F32)

    @pl.when(j < n_norm_tiles)
    def _():
        for s in range(tn // HEAD_DIM):
            sl = slice(s * HEAD_DIM, (s + 1) * HEAD_DIM)
            seg = acc[:, sl]
            r = lax.rsqrt(jnp.mean(seg * seg, axis=-1, keepdims=True) + NORM_EPS)
            o_ref[:, sl] = (seg * r * hg_ref[:, sl]).astype(o_ref.dtype)

    @pl.when(j >= n_norm_tiles)
    def _():
        o_ref[...] = acc.astype(o_ref.dtype)


def _norm_proj(h2, norm_gain, w, head_gain, n_norm_cols, *, tm=512, tn=512):
    T, D = h2.shape
    N = w.shape[1]
    assert T % tm == 0 and N % tn == 0 and n_norm_cols % tn == 0
    return pl.pallas_call(
        functools.partial(_norm_proj_kernel, n_norm_tiles=n_norm_cols // tn, tn=tn),
        grid=(T // tm, N // tn),
        in_specs=[pl.BlockSpec((tm, D), lambda i, j: (i, 0)),
                  pl.BlockSpec((1, D), lambda i, j: (0, 0)),
                  pl.BlockSpec((D, tn), lambda i, j: (0, j)),
                  pl.BlockSpec((1, tn), lambda i, j: (0, j))],
        out_specs=pl.BlockSpec((tm, tn), lambda i, j: (i, j)),
        out_shape=jax.ShapeDtypeStruct((T, N), BF16),
        scratch_shapes=[pltpu.VMEM((tm, D), BF16)],
        compiler_params=_cparams(("parallel", "arbitrary")),
        name="norm_proj",
    )(h2, norm_gain.reshape(1, D), w, head_gain.reshape(1, N))


def _proj_res_kernel(a_ref, w_ref, r_ref, o_ref):
    o_ref[...] = r_ref[...] + jnp.dot(a_ref[...], w_ref[...], preferred_element_type=F32)


def _proj_residual(a, w, res, *, tm=512, tn=512):
    T, K = a.shape
    N = w.shape[1]
    return pl.pallas_call(
        _proj_res_kernel,
        grid=(T // tm, N // tn),
        in_specs=[pl.BlockSpec((tm, K), lambda i, j: (i, 0)),
                  pl.BlockSpec((K, tn), lambda i, j: (0, j)),
                  pl.BlockSpec((tm, tn), lambda i, j: (i, j))],
        out_specs=pl.BlockSpec((tm, tn), lambda i, j: (i, j)),
        out_shape=jax.ShapeDtypeStruct((T, N), F32),
        compiler_params=_cparams(("parallel", "arbitrary")),
        name="proj_residual",
    )(a, w, res)


def _ffn_kernel(h_ref, g_ref, wg_ref, wu_ref, wd_ref, o_ref, hn_ref):
    f = pl.program_id(1)

    @pl.when(f == 0)
    def _():
        x = h_ref[...]
        r = lax.rsqrt(jnp.mean(x * x, axis=-1, keepdims=True) + NORM_EPS)
        hn_ref[...] = (x * r * g_ref[...]).astype(BF16)
        o_ref[...] = x

    hn = hn_ref[...]
    gate = jnp.dot(hn, wg_ref[...], preferred_element_type=F32)
    up = jnp.dot(hn, wu_ref[...], preferred_element_type=F32)
    act = (gate * jax.nn.sigmoid(gate) * up).astype(BF16)
    o_ref[...] += jnp.dot(act, wd_ref[...], preferred_element_type=F32)


def _dense_ffn(h2, norm_gain, wg, wu, wd, *, tm=1024, tf=512):
    T, D = h2.shape
    F = wg.shape[1]
    return pl.pallas_call(
        _ffn_kernel,
        grid=(T // tm, F // tf),
        in_specs=[pl.BlockSpec((tm, D), lambda i, f: (i, 0)),
                  pl.BlockSpec((1, D), lambda i, f: (0, 0)),
                  pl.BlockSpec((D, tf), lambda i, f: (0, f)),
                  pl.BlockSpec((D, tf), lambda i, f: (0, f)),
                  pl.BlockSpec((tf, D), lambda i, f: (f, 0))],
        out_specs=pl.BlockSpec((tm, D), lambda i, f: (i, 0)),
        out_shape=jax.ShapeDtypeStruct((T, D), F32),
        scratch_shapes=[pltpu.VMEM((tm, D), BF16)],
        compiler_params=_cparams(("parallel", "arbitrary")),
        name="dense_ffn",
    )(h2, norm_gain.reshape(1, D), wg, wu, wd)


def _router_kernel(h_ref, g_ref, wr_ref, br_ref, gate_ref):
    x = h_ref[...]
    r = lax.rsqrt(jnp.mean(x * x, axis=-1, keepdims=True) + NORM_EPS)
    hn = x * r * g_ref[...]
    logits = jnp.dot(hn, wr_ref[...], preferred_element_type=F32,
                     precision=lax.Precision.HIGHEST) + br_ref[...]
    lane = lax.broadcasted_iota(jnp.int32, logits.shape, 1)
    logits = jnp.where(lane < N_EXPERTS, logits, NEG_INF)
    m1 = jnp.max(logits, axis=-1, keepdims=True)
    i1 = jnp.min(jnp.where(logits == m1, lane, LANES), axis=-1, keepdims=True)
    rest = jnp.where(lane == i1, NEG_INF, logits)
    m2 = jnp.max(rest, axis=-1, keepdims=True)
    i2 = jnp.min(jnp.where(rest == m2, lane, LANES), axis=-1, keepdims=True)
    e2 = jnp.exp(m2 - m1)
    w1 = 1.0 / (1.0 + e2)
    w2 = e2 / (1.0 + e2)
    gate_ref[...] = jnp.where(lane == i1, w1, 0.0) + jnp.where(lane == i2, w2, 0.0)


def _router(h2, norm_gain, w_router, b_router, *, tm=512):
    T, D = h2.shape
    wr = jnp.zeros((D, LANES), F32).at[:, :N_EXPERTS].set(w_router)
    br = jnp.zeros((1, LANES), F32).at[0, :N_EXPERTS].set(b_router)
    return pl.pallas_call(
        _router_kernel,
        grid=(T // tm,),
        in_specs=[pl.BlockSpec((tm, D), lambda i: (i, 0)),
                  pl.BlockSpec((1, D), lambda i: (0, 0)),
                  pl.BlockSpec((D, LANES), lambda i: (0, 0)),
                  pl.BlockSpec((1, LANES), lambda i: (0, 0))],
        out_specs=pl.BlockSpec((tm, LANES), lambda i: (i, 0)),
        out_shape=jax.ShapeDtypeStruct((T, LANES), F32),
        compiler_params=_cparams(("parallel",)),
        name="moe_router",
    )(h2, norm_gain.reshape(1, D), wr, br)


def _moe_dense_kernel(h_ref, g_ref, gate_ref, wg_ref, wu_ref, wd_ref, o_ref, hn_ref):
    e = pl.program_id(1)
    f = pl.program_id(2)

    @pl.when((e == 0) & (f == 0))
    def _():
        x = h_ref[...]
        r = lax.rsqrt(jnp.mean(x * x, axis=-1, keepdims=True) + NORM_EPS)
        hn_ref[...] = (x * r * g_ref[...]).astype(BF16)
        o_ref[...] = x

    lane = lax.broadcasted_iota(jnp.int32, gate_ref.shape, 1)
    gcol = jnp.sum(jnp.where(lane == e, gate_ref[...], 0.0), axis=-1, keepdims=True)
    hn = hn_ref[...]
    gate = jnp.dot(hn, wg_ref[0], preferred_element_type=F32)
    up = jnp.dot(hn, wu_ref[0], preferred_element_type=F32)
    act = (gate * jax.nn.sigmoid(gate) * up).astype(BF16)
    o_ref[...] += gcol * jnp.dot(act, wd_ref[0], preferred_element_type=F32)


def _moe_ffn(h2, norm_gain, w_router, b_router, we_gate, we_up, we_down, *, tm=1024, tf=512):
    T, D = h2.shape
    E, _, F = we_gate.shape
    gates = _router(h2, norm_gain, w_router, b_router)
    return pl.pallas_call(
        _moe_dense_kernel,
        grid=(T // tm, E, F // tf),
        in_specs=[pl.BlockSpec((tm, D), lambda i, e, f: (i, 0)),
                  pl.BlockSpec((1, D), lambda i, e, f: (0, 0)),
                  pl.BlockSpec((tm, LANES), lambda i, e, f: (i, 0)),
                  pl.BlockSpec((1, D, tf), lambda i, e, f: (e, 0, f)),
                  pl.BlockSpec((1, D, tf), lambda i, e, f: (e, 0, f)),
                  pl.BlockSpec((1, tf, D), lambda i, e, f: (e, f, 0))],
        out_specs=pl.BlockSpec((tm, D), lambda i, e, f: (i, 0)),
        out_shape=jax.ShapeDtypeStruct((T, D), F32),
        scratch_shapes=[pltpu.VMEM((tm, D), BF16)],
        compiler_params=_cparams(("parallel", "arbitrary", "arbitrary")),
        name="moe_experts",
    )(h2, norm_gain.reshape(1, D), gates, we_gate, we_up, we_down)


def _rel_bucket(dist):
    n = jnp.maximum(dist, 0)
    max_exact = REL_BUCKETS // 2
    nf = jnp.maximum(n, max_exact).astype(F32)
    large = max_exact + (jnp.log(nf / max_exact) / math.log(REL_MAX_DIST / max_exact)
                         * (REL_BUCKETS - max_exact)).astype(jnp.int32)
    return jnp.where(n < max_exact, n, jnp.minimum(large, REL_BUCKETS - 1))


def _bias_tiles_kernel(tab_ref, idx_ref, o_ref):
    t = idx_ref.shape[-1]

    def chunk(c, carry):
        rows = pl.ds(pl.multiple_of(c * 8, 8), 8)
        idx = idx_ref[0, rows, :]
        for h in range(N_SCORE_HEADS):
            tile = jnp.zeros((8, t), F32)
            for b in range(REL_BUCKETS):
                tile = jnp.where(idx == b, tab_ref[h, b], tile)
            o_ref[h, 0, rows, :] = tile
        return carry

    lax.fori_loop(0, t // 8, chunk, 0)


def _bias_tiles(rel_bias, S, t):
    n_delta = S // t
    i = jnp.arange(t)
    dist = (jnp.arange(n_delta) * t)[:, None, None] + i[None, :, None] - i[None, None, :]
    idx = _rel_bucket(dist).astype(jnp.int32)
    return pl.pallas_call(
        _bias_tiles_kernel,
        grid=(n_delta,),
        in_specs=[pl.BlockSpec(memory_space=pltpu.SMEM),
                  pl.BlockSpec((1, t, t), lambda d: (d, 0, 0))],
        out_specs=pl.BlockSpec((N_SCORE_HEADS, 1, t, t), lambda d: (0, d, 0, 0)),
        out_shape=jax.ShapeDtypeStruct((N_SCORE_HEADS, n_delta, t, t), F32),
        compiler_params=_cparams(("parallel",)),
        name="bias_tiles",
    )(rel_bias.T, idx)


def _flash_step(carry, s, mask, v):
    m, l, acc = carry
    if mask is not None:
        s = jnp.where(mask, s, NEG_INF)
    m_new = jnp.maximum(m, jnp.max(s, axis=-1, keepdims=True))
    alpha = jnp.exp(m - m_new)
    p = jnp.exp(s - m_new)
    if mask is not None:
        p = jnp.where(mask, p, 0.0)
    l = alpha * l + jnp.sum(p, axis=-1, keepdims=True)
    acc = alpha * acc + jnp.dot(p.astype(BF16), v, preferred_element_type=F32)
    return m_new, l, acc


def _flash_init(rows, dv):
    return (jnp.full((rows, 1), NEG_INF, F32), jnp.zeros((rows, 1), F32), jnp.zeros((rows, dv), F32))


def _flash_out(carry):
    _, l, acc = carry
    return acc / jnp.maximum(l, 1e-30)


def _tile_rows(ref, kb, t):
    return ref[0, pl.ds(pl.multiple_of(kb * t, t), t), :]


def _diff_attn_kernel(lam_ref, sub_ref, q_ref, k_ref, v_ref, bias_ref, o_ref, *, lambda_init):
    t = ATT_TILE
    qb = pl.program_id(2)
    q = q_ref[0]
    lane = lax.broadcasted_iota(jnp.int32, (1, LANES), 1)
    causal = (lax.broadcasted_iota(jnp.int32, (t, t), 0) >= lax.broadcasted_iota(jnp.int32, (t, t), 1))
    outs = []
    for mi in range(2):
        qm = jnp.where((lane >= mi * HEAD_DIM) & (lane < (mi + 1) * HEAD_DIM), q, jnp.zeros_like(q))

        def body(kb, carry, qm=qm, mi=mi):
            s = _dot_nt(qm, _tile_rows(k_ref, kb, t)) + bias_ref[mi, qb - kb]
            return _flash_step(carry, s, None, _tile_rows(v_ref, kb, t))

        carry = lax.fori_loop(0, qb, body, _flash_init(t, LANES))
        s = _dot_nt(qm, _tile_rows(k_ref, qb, t)) + bias_ref[mi, 0]
        outs.append(_flash_out(_flash_step(carry, s, causal, _tile_rows(v_ref, qb, t))))
    lv = lam_ref[...]
    lam = (jnp.exp(jnp.sum(lv[0:1] * lv[1:2], axis=-1, keepdims=True))
           - jnp.exp(jnp.sum(lv[2:3] * lv[3:4], axis=-1, keepdims=True)) + lambda_init)
    o = outs[0] - lam * outs[1]
    r = lax.rsqrt(jnp.mean(o * o, axis=-1, keepdims=True) + NORM_EPS)
    o_ref[0] = (o * r * sub_ref[...] * (1.0 - lambda_init)).astype(o_ref.dtype)


def _diff_attention(qkv, lam4, subln_gain, bias_tiles, lambda_init):
    B, S, _ = qkv.shape
    t = ATT_TILE
    nq = S // t
    H = DIFF_HEADS
    return pl.pallas_call(
        functools.partial(_diff_attn_kernel, lambda_init=lambda_init),
        grid=(H, B, nq),
        in_specs=[pl.BlockSpec((4, HEAD_DIM), lambda h, b, i: (0, 0)),
                  pl.BlockSpec((1, LANES), lambda h, b, i: (0, 0)),
                  pl.BlockSpec((1, t, LANES), lambda h, b, i: (b, i, h)),
                  pl.BlockSpec((1, S, LANES), lambda h, b, i: (b, 0, H + h)),
                  pl.BlockSpec((1, S, LANES), lambda h, b, i: (b, 0, 2 * H + h)),
                  pl.BlockSpec((2, nq, t, t), lambda h, b, i: (h, 0, 0, 0))],
        out_specs=pl.BlockSpec((1, t, LANES), lambda h, b, i: (b, i, h)),
        out_shape=jax.ShapeDtypeStruct((B, S, H * LANES), BF16),
        compiler_params=_cparams(("parallel", "parallel", "arbitrary")),
        name="diff_attention",
    )(lam4, subln_gain.reshape(1, LANES), qkv, qkv, qkv, bias_tiles)


def _rank_select(score, cand_valid, topk):
    C = score.shape[-1]
    col = lax.broadcasted_iota(jnp.int32, score.shape, 1)
    rank = jnp.zeros(score.shape, F32)
    for c in range(C):
        sc = score[:, c:c + 1]
        tie = (col > c).astype(F32)
        rank = rank + jnp.where(sc > score, 1.0, jnp.where(sc == score, tie, 0.0))
    return jnp.where(cand_valid & (rank < topk), 1.0, 0.0)


def _moba_kernel(q_ref, k_ref, v_ref, bias_ref, o_ref, kmean_ref):
    t = ATT_TILE
    qb = pl.program_id(2)
    n_blk = kmean_ref.shape[0]

    @pl.when(qb == 0)
    def _():
        for n in range(n_blk):
            kb = k_ref[0, n * t:(n + 1) * t, :].astype(F32)
            kmean_ref[n:n + 1, :] = jnp.mean(kb, axis=0, keepdims=True)

    q = q_ref[0]
    lane = lax.broadcasted_iota(jnp.int32, (1, LANES), 1)
    causal = (lax.broadcasted_iota(jnp.int32, (t, t), 0) >= lax.broadcasted_iota(jnp.int32, (t, t), 1))
    blk = lax.broadcasted_iota(jnp.int32, (t, n_blk), 1)
    km = kmean_ref[...]
    km_hi = km.astype(BF16)
    km_lo = (km - km_hi.astype(F32)).astype(BF16)
    out = jnp.zeros((t, LANES), F32)
    for hj in range(2):
        in_head = (lane >= hj * HEAD_DIM) & (lane < (hj + 1) * HEAD_DIM)
        qm = jnp.where(in_head, q, jnp.zeros_like(q))
        gate = _dot_nt(qm, km_hi) + _dot_nt(qm, km_lo)
        past = blk < qb
        sel = _rank_select(jnp.where(past, gate, NEG_INF), past, MOBA_TOPK)

        def body(kb, carry, qm=qm, hj=hj, sel=sel):
            s = _dot_nt(qm, _tile_rows(k_ref, kb, t)) + bias_ref[hj, qb - kb]
            chosen = jnp.sum(jnp.where(blk == kb, sel, 0.0), axis=-1, keepdims=True) > 0.5
            return _flash_step(carry, s, jnp.broadcast_to(chosen, s.shape), _tile_rows(v_ref, kb, t))

        carry = lax.fori_loop(0, qb, body, _flash_init(t, LANES))
        s = _dot_nt(qm, _tile_rows(k_ref, qb, t)) + bias_ref[hj, 0]
        o = _flash_out(_flash_step(carry, s, causal, _tile_rows(v_ref, qb, t)))
        out = jnp.where(in_head, o, out)
    o_ref[0] = out.astype(o_ref.dtype)


def _moba_attention(qkv, bias_tiles):
    B, S, _ = qkv.shape
    t = ATT_TILE
    assert t == MOBA_BLOCK and S % t == 0
    nq = S // t
    HP = D_MODEL // LANES
    return pl.pallas_call(
        _moba_kernel,
        grid=(HP, B, nq),
        in_specs=[pl.BlockSpec((1, t, LANES), lambda h, b, i: (b, i, h)),
                  pl.BlockSpec((1, S, LANES), lambda h, b, i: (b, 0, HP + h)),
                  pl.BlockSpec((1, S, LANES), lambda h, b, i: (b, 0, 2 * HP + h)),
                  pl.BlockSpec((2, nq, t, t), lambda h, b, i: (h, 0, 0, 0))],
        out_specs=pl.BlockSpec((1, t, LANES), lambda h, b, i: (b, i, h)),
        out_shape=jax.ShapeDtypeStruct((B, S, D_MODEL), BF16),
        scratch_shapes=[pltpu.VMEM((S // t, LANES), F32)],
        compiler_params=_cparams(("parallel", "parallel", "arbitrary")),
        name="moba_attention",
    )(qkv, qkv, qkv, bias_tiles)


def _compress_kernel(x_ref, pos_ref, w1_ref, w2_ref, gain_ref, o_ref, *, normalize):
    G, n_half, width = x_ref.shape[1:]
    x = x_ref[0].reshape(G * n_half, width).astype(F32)
    first = (x + pos_ref[0:1, :]).astype(BF16)
    second = (x + pos_ref[1:2, :]).astype(BF16)
    u = jnp.dot(first, w1_ref[0:width, :], preferred_element_type=F32)
    low = jnp.dot(second, w1_ref[width:2 * width, :], preferred_element_type=F32)
    rows = G * n_half
    hid = u + pltpu.roll(low, rows - 1, 0)
    c = jnp.dot(jax.nn.gelu(hid).astype(BF16), w2_ref[...], preferred_element_type=F32)
    if normalize:
        r = lax.rsqrt(jnp.mean(c * c, axis=-1, keepdims=True) + NORM_EPS)
        c = c * r * gain_ref[...]
    o_ref[0] = c.reshape(G, n_half, HEAD_DIM).astype(o_ref.dtype)


def _compress(x, pos, w1, w2, gain, normalize):
    B, G, S, d = x.shape
    n_half = S // NSA_CMP_STRIDE
    width = NSA_CMP_STRIDE * d
    return pl.pallas_call(
        functools.partial(_compress_kernel, normalize=normalize),
        grid=(B,),
        in_specs=[pl.BlockSpec((1, G, n_half, width), lambda b: (b, 0, 0, 0)),
                  pl.BlockSpec((2, width), lambda b: (0, 0)),
                  pl.BlockSpec((2 * width, NSA_CMP_HIDDEN), lambda b: (0, 0)),
                  pl.BlockSpec((NSA_CMP_HIDDEN, d), lambda b: (0, 0)),
                  pl.BlockSpec((1, d), lambda b: (0, 0))],
        out_specs=pl.BlockSpec((1, G, n_half, d), lambda b: (b, 0, 0, 0)),
        out_shape=jax.ShapeDtypeStruct((B, G, n_half, d), BF16),
        compiler_params=_cparams(("parallel",)),
        name="nsa_compress",
    )(x.reshape(B, G, n_half, width), pos.reshape(2, width), w1, w2, gain.reshape(1, d))


def _nsa_kernel(q_ref, kc_ref, vc_ref, ks_ref, vs_ref, kw_ref, vw_ref, g_ref, bias_ref,
                ov_ref, ex_ref, o_ref, sel_ref):
    t = ATT_TILE
    R = NSA_GROUP_SIZE
    M = R * t
    qb = pl.program_id(2)
    n_cmp_rows = kc_ref.shape[2]
    n_sel = ov_ref.shape[1]
    qs = q_ref[0].reshape(M, HEAD_DIM)
    qpos = qb * t + lax.broadcasted_iota(jnp.int32, (t, 1), 0)

    def heads(mask2d):
        return jnp.broadcast_to(mask2d[None], (R,) + mask2d.shape).reshape(M, mask2d.shape[-1])

    n_idx = lax.broadcasted_iota(jnp.int32, (t, n_cmp_rows), 1)
    cmask = heads((n_idx * NSA_CMP_STRIDE + (NSA_CMP_BLOCK - 1) <= qpos) & (n_idx < n_cmp_rows - 1))
    cl = jnp.where(cmask, _dot_nt(qs, kc_ref[0, 0]), NEG_INF)
    cm = jnp.max(cl, axis=-1, keepdims=True)
    cp = jnp.where(cmask, jnp.exp(cl - cm), 0.0)
    cp = cp / jnp.maximum(jnp.sum(cp, axis=-1, keepdims=True), 1e-30)
    o_cmp = jnp.dot(cp.astype(BF16), vc_ref[0, 0], preferred_element_type=F32)

    psum = cp[0:t]
    for r in range(1, R):
        psum = psum + cp[r * t:(r + 1) * t]
    p_hi = psum.astype(BF16)
    p_lo = (psum - p_hi.astype(F32)).astype(BF16)
    imp = (jnp.dot(p_hi, ov_ref[...], preferred_element_type=F32)
           + jnp.dot(p_lo, ov_ref[...], preferred_element_type=F32))
    blk = lax.broadcasted_iota(jnp.int32, (t, n_sel), 1)
    cur = qpos // NSA_SEL_BLOCK
    forced = (blk == 0) | (blk == cur) | (blk == cur - 1)
    score = jnp.where(forced, FORCE_SCORE, jnp.where(blk <= cur, imp, NEG_INF))
    sel = _rank_select(score, blk <= cur, NSA_SEL_TOPK)
    sel_ref[...] = jnp.dot(sel.astype(BF16), ex_ref[...], preferred_element_type=F32)

    causal = (lax.broadcasted_iota(jnp.int32, (t, t), 0) >= lax.broadcasted_iota(jnp.int32, (t, t), 1))

    def key_cols(kb):
        return pl.ds(pl.multiple_of(kb * t, t), t)

    def sel_body(kb, carry):
        s = _dot_nt(qs, ks_ref[0, 0, key_cols(kb), :]) + bias_ref[:, qb - kb].reshape(M, t)
        chosen = sel_ref[:, key_cols(kb)] > 0.5
        return _flash_step(carry, s, heads(chosen), vs_ref[0, 0, key_cols(kb), :])

    carry = lax.fori_loop(0, qb, sel_body, _flash_init(M, HEAD_DIM))
    s = _dot_nt(qs, ks_ref[0, 0, key_cols(qb), :]) + bias_ref[:, 0].reshape(M, t)
    chosen = (sel_ref[:, key_cols(qb)] > 0.5) & causal
    o_sel = _flash_out(_flash_step(carry, s, heads(chosen), vs_ref[0, 0, key_cols(qb), :]))

    row = lax.broadcasted_iota(jnp.int32, (t, t), 0)
    col = lax.broadcasted_iota(jnp.int32, (t, t), 1)
    carry = _flash_init(M, HEAD_DIM)
    n_back = NSA_WINDOW // t
    for back in range(n_back, -1, -1):
        kb = jnp.maximum(qb - back, 0)
        dist = back * t + row - col
        wmask = (dist >= 0) & (dist < NSA_WINDOW) & (qb >= back)
        s = _dot_nt(qs, kw_ref[0, 0, key_cols(kb), :]) + bias_ref[:, back].reshape(M, t)
        carry = _flash_step(carry, s, heads(wmask), vw_ref[0, 0, key_cols(kb), :])
    o_win = _flash_out(carry)

    gates = jax.nn.sigmoid(g_ref[0, 0].astype(F32))
    for r in range(R):
        rows = slice(r * t, (r + 1) * t)
        o = (gates[:, 3 * r:3 * r + 1] * o_cmp[rows]
             + gates[:, 3 * r + 1:3 * r + 2] * o_sel[rows]
             + gates[:, 3 * r + 2:3 * r + 3] * o_win[rows])
        o_ref[0, r] = o.astype(o_ref.dtype)


def _nsa_constants(S):
    n_cmp_rows = S // NSA_CMP_STRIDE
    n_sel = S // NSA_SEL_BLOCK
    cmp_start = jnp.arange(n_cmp_rows) * NSA_CMP_STRIDE
    sel_start = jnp.arange(n_sel) * NSA_SEL_BLOCK
    overlap = jnp.clip(jnp.minimum(cmp_start[:, None] + NSA_CMP_BLOCK, sel_start[None, :] + NSA_SEL_BLOCK)
                       - jnp.maximum(cmp_start[:, None], sel_start[None, :]), 0).astype(F32) / NSA_CMP_STRIDE
    overlap = jnp.where(cmp_start[:, None] + NSA_CMP_BLOCK <= S, overlap, 0.0)
    expand = (jnp.arange(S)[None, :] // NSA_SEL_BLOCK == jnp.arange(n_sel)[:, None])
    return overlap.astype(BF16), expand.astype(BF16)


def _nsa_attention(q, kc, vc, ks, vs, kw, vw, gates, bias_tiles):
    B, Hh, S, d = q.shape
    t = ATT_TILE
    nq = S // t
    G, R = NSA_KV_GROUPS, NSA_GROUP_SIZE
    n_cmp_rows = kc.shape[2]
    overlap, expand = _nsa_constants(S)
    n_sel = overlap.shape[1]
    kv_spec = pl.BlockSpec((1, 1, S, d), lambda g, b, i: (b, g, 0, 0))
    cmp_spec = pl.BlockSpec((1, 1, n_cmp_rows, d), lambda g, b, i: (b, g, 0, 0))
    return pl.pallas_call(
        _nsa_kernel,
        grid=(G, B, nq),
        in_specs=[pl.BlockSpec((1, R, t, d), lambda g, b, i: (b, g, i, 0)),
                  cmp_spec, cmp_spec, kv_spec, kv_spec, kv_spec, kv_spec,
                  pl.BlockSpec((1, 1, t, 3 * R), lambda g, b, i: (b, g, i, 0)),
                  pl.BlockSpec((R, nq, t, t), lambda g, b, i: (g, 0, 0, 0)),
                  pl.BlockSpec((n_cmp_rows, n_sel), lambda g, b, i: (0, 0)),
                  pl.BlockSpec((n_sel, S), lambda g, b, i: (0, 0))],
        out_specs=pl.BlockSpec((1, R, t, d), lambda g, b, i: (b, g, i, 0)),
        out_shape=jax.ShapeDtypeStruct((B, Hh, S, d), BF16),
        scratch_shapes=[pltpu.VMEM((t, S), F32)],
        compiler_params=_cparams(("parallel", "parallel", "arbitrary")),
        name="nsa_attention",
    )(q, kc, vc, ks, vs, kw, vw, gates, bias_tiles, overlap, expand)


def _qkv_head_gain(q_gain, k_gain, n_heads, n_plain):
    return jnp.concatenate([jnp.tile(q_gain, n_heads) * ATTN_SCALE, jnp.tile(k_gain, n_heads),
                            jnp.ones((n_plain,), F32)])


def _diff_layer(h2, B, S, norm_mix, p, bias_tiles, lambda_init):
    w_in, w_out, q_gain, k_gain, lq1, lk1, lq2, lk2, subln = p
    hg = _qkv_head_gain(q_gain, k_gain, 2 * DIFF_HEADS, D_MODEL)
    qkv = _norm_proj(h2, norm_mix, w_in.astype(BF16), hg, 2 * D_MODEL)
    o = _diff_attention(qkv.reshape(B, S, -1), jnp.stack([lq1, lk1, lq2, lk2]), subln, bias_tiles, lambda_init)
    return _proj_residual(o.reshape(B * S, -1), w_out.astype(BF16), h2)


def _moba_layer(h2, B, S, norm_mix, p, bias_tiles):
    w_in, w_out, q_gain, k_gain = p
    hg = _qkv_head_gain(q_gain, k_gain, N_SCORE_HEADS, D_MODEL)
    qkv = _norm_proj(h2, norm_mix, w_in.astype(BF16), hg, 2 * D_MODEL)
    o = _moba_attention(qkv.reshape(B, S, -1), bias_tiles)
    return _proj_residual(o.reshape(B * S, -1), w_out.astype(BF16), h2)


def _nsa_layer(h2, B, S, norm_mix, p, bias_tiles):
    w_in, w_out, q_gain, k_gain, pos_k, pos_v, k_w1, k_w2, v_w1, v_w2 = p
    G, d = NSA_KV_GROUPS, HEAD_DIM
    kvw = G * d
    qw = N_SCORE_HEADS * d
    sec = {name: slice(qw + i * kvw, qw + (i + 1) * kvw) for i, name in
           enumerate(("kc", "vc", "ks", "vs", "kw", "vw"))}
    n_gate = 3 * N_SCORE_HEADS
    tn = 384
    n_used = qw + 6 * kvw + n_gate
    n_pad = -n_used % tn
    w_perm = jnp.concatenate([w_in[:, :qw], w_in[:, sec["ks"]], w_in[:, sec["kw"]], w_in[:, sec["kc"]],
                              w_in[:, sec["vc"]], w_in[:, sec["vs"]], w_in[:, sec["vw"]],
                              w_in[:, qw + 6 * kvw:], jnp.zeros((D_MODEL, n_pad), F32)], axis=1)
    hg = jnp.concatenate([jnp.tile(q_gain, N_SCORE_HEADS) * ATTN_SCALE, jnp.tile(k_gain[1], G),
                          jnp.tile(k_gain[2], G), jnp.ones((n_used + n_pad - qw - 2 * kvw,), F32)])
    proj = _norm_proj(h2, norm_mix, w_perm.astype(BF16), hg, qw + 2 * kvw, tn=tn).reshape(B, S, -1)

    def grouped(lo, width, n):
        return proj[:, :, lo:lo + width].reshape(B, S, n, width // n).transpose(0, 2, 1, 3)

    q = grouped(0, qw, N_SCORE_HEADS)
    ks, kw, kc, vc, vs, vw = (grouped(qw + i * kvw, kvw, G) for i in range(6))
    gates = grouped(qw + 6 * kvw, n_gate, G)
    k_cmp = _compress(kc, pos_k, k_w1.astype(BF16), k_w2.astype(BF16), k_gain[0], True)
    v_cmp = _compress(vc, pos_v, v_w1.astype(BF16), v_w2.astype(BF16), k_gain[0], False)
    o = _nsa_attention(q, k_cmp, v_cmp, ks, vs, kw, vw, gates, bias_tiles)
    o = o.transpose(0, 2, 1, 3).reshape(B * S, qw)
    return _proj_residual(o, w_out.astype(BF16), h2)


def _diff_lambda_init(layer):
    return 0.8 - 0.6 * math.exp(-0.3 * layer)


def kernel(x, rel_bias,
           l0_norm_mix, l0_w_in, l0_w_out, l0_q_gain, l0_k_gain, l0_lam_q1, l0_lam_k1, l0_lam_q2, l0_lam_k2,
           l0_subln_gain, l0_norm_ffn, l0_w_gate, l0_w_up, l0_w_down,
           l1_norm_mix, l1_w_in, l1_w_out, l1_q_gain, l1_k_gain, l1_norm_ffn, l1_w_router, l1_b_router,
           l1_we_gate, l1_we_up, l1_we_down,
           l2_norm_mix, l2_w_in, l2_w_out, l2_q_gain, l2_k_gain, l2_cmp_pos_k, l2_cmp_pos_v, l2_cmp_k_w1,
           l2_cmp_k_w2, l2_cmp_v_w1, l2_cmp_v_w2, l2_norm_ffn, l2_w_gate, l2_w_up, l2_w_down,
           l3_norm_mix, l3_w_in, l3_w_out, l3_q_gain, l3_k_gain, l3_lam_q1, l3_lam_k1, l3_lam_q2, l3_lam_k2,
           l3_subln_gain, l3_norm_ffn, l3_w_router, l3_b_router, l3_we_gate, l3_we_up, l3_we_down):
    B, S, D = x.shape
    bias_tiles = _bias_tiles(rel_bias, S, ATT_TILE)
    h = x.reshape(B * S, D)

    h = _diff_layer(h, B, S, l0_norm_mix, (l0_w_in, l0_w_out, l0_q_gain, l0_k_gain, l0_lam_q1, l0_lam_k1,
                                           l0_lam_q2, l0_lam_k2, l0_subln_gain), bias_tiles, _diff_lambda_init(0))
    h = _dense_ffn(h, l0_norm_ffn, l0_w_gate.astype(BF16), l0_w_up.astype(BF16), l0_w_down.astype(BF16))

    h = _moba_layer(h, B, S, l1_norm_mix, (l1_w_in, l1_w_out, l1_q_gain, l1_k_gain), bias_tiles)
    h = _moe_ffn(h, l1_norm_ffn, l1_w_router, l1_b_router, l1_we_gate.astype(BF16), l1_we_up.astype(BF16),
                 l1_we_down.astype(BF16))

    h = _nsa_layer(h, B, S, l2_norm_mix, (l2_w_in, l2_w_out, l2_q_gain, l2_k_gain, l2_cmp_pos_k, l2_cmp_pos_v,
                                          l2_cmp_k_w1, l2_cmp_k_w2, l2_cmp_v_w1, l2_cmp_v_w2), bias_tiles)
    h = _dense_ffn(h, l2_norm_ffn, l2_w_gate.astype(BF16), l2_w_up.astype(BF16), l2_w_down.astype(BF16))

    h = _diff_layer(h, B, S, l3_norm_mix, (l3_w_in, l3_w_out, l3_q_gain, l3_k_gain, l3_lam_q1, l3_lam_k1,
                                           l3_lam_q2, l3_lam_k2, l3_subln_gain), bias_tiles, _diff_lambda_init(3))
    h = _moe_ffn(h, l3_norm_ffn, l3_w_router, l3_b_router, l3_we_gate.astype(BF16), l3_we_up.astype(BF16),
                 l3_we_down.astype(BF16))
    return h.reshape(B, S, D)
```

```python
import functools
import math

import jax
import jax.numpy as jnp
from jax import lax
from jax.experimental import pallas as pl
from jax.experimental.pallas import tpu as pltpu

F32 = jnp.float32
BF16 = jnp.bfloat16

D_MODEL = 1024
HEAD_DIM = 64
N_SCORE_HEADS = 16
DIFF_HEADS = 8
MOBA_BLOCK = 256
MOBA_TOPK = 3
NSA_KV_GROUPS = 4
NSA_GROUP_SIZE = 4
NSA_CMP_BLOCK = 32
NSA_CMP_STRIDE = 16
NSA_CMP_HIDDEN = 256
NSA_SEL_BLOCK = 64
NSA_SEL_TOPK = 16
NSA_WINDOW = 512
REL_BUCKETS = 32
REL_MAX_DIST = 1024
FFN_DIM = 3584
N_EXPERTS = 8
NORM_EPS = 1e-6
NEG_INF = -1e30
FORCE_SCORE = 1e30
ATTN_SCALE = HEAD_DIM ** -0.5
LOG2E = math.log2(math.e)
Q_SCALE = ATTN_SCALE * LOG2E

LANES = 128
ATT_TILE = 256
VMEM_LIMIT = 56 * 1024 * 1024


def _cparams(sem):
    return pltpu.CompilerParams(dimension_semantics=sem, vmem_limit_bytes=VMEM_LIMIT)


def _dot_nt(a, b):
    return lax.dot_general(a, b, (((1,), (1,)), ((), ())), preferred_element_type=F32)


def _norm_proj_kernel(x_ref, g_ref, w_ref, hg_ref, o_ref, xn_ref, *, n_norm_tiles, tn):
    j = pl.program_id(1)

    @pl.when(j == 0)
    def _():
        x = x_ref[...]
        r = lax.rsqrt(jnp.mean(x * x, axis=-1, keepdims=True) + NORM_EPS)
        xn_ref[...] = (x * r * g_ref[...]).astype(BF16)

    acc = jnp.dot(xn_ref[...], w_ref[...], preferred_element_type=F32)

    @pl.when(j < n_norm_tiles)
    def _():
        for s in range(tn // HEAD_DIM):
            sl = slice(s * HEAD_DIM, (s + 1) * HEAD_DIM)
            seg = acc[:, sl]
            r = lax.rsqrt(jnp.mean(seg * seg, axis=-1, keepdims=True) + NORM_EPS)
            o_ref[:, sl] = (seg * r * hg_ref[:, sl]).astype(o_ref.dtype)

    @pl.when(j >= n_norm_tiles)
    def _():
        o_ref[...] = acc.astype(o_ref.dtype)


def _norm_proj(h2, norm_gain, w, head_gain, n_norm_cols, *, tm=512, tn=512):
    T, D = h2.shape
    N = w.shape[1]
    assert T % tm == 0 and N % tn == 0 and n_norm_cols % tn == 0
    return pl.pallas_call(
        functools.partial(_norm_proj_kernel, n_norm_tiles=n_norm_cols // tn, tn=tn),
        grid=(T // tm, N // tn),
        in_specs=[pl.BlockSpec((tm, D), lambda i, j: (i, 0)),
                  pl.BlockSpec((1, D), lambda i, j: (0, 0)),
                  pl.BlockSpec((D, tn), lambda i, j: (0, j)),
                  pl.BlockSpec((1, tn), lambda i, j: (0, j))],
        out_specs=pl.BlockSpec((tm, tn), lambda i, j: (i, j)),
        out_shape=jax.ShapeDtypeStruct((T, N), BF16),
        scratch_shapes=[pltpu.VMEM((tm, D), BF16)],
        compiler_params=_cparams(("parallel", "arbitrary")),
        name="norm_proj",
    )(h2, norm_gain.reshape(1, D), w, head_gain.reshape(1, N))


def _proj_res_kernel(a_ref, w_ref, r_ref, o_ref):
    o_ref[...] = r_ref[...] + jnp.dot(a_ref[...], w_ref[...], preferred_element_type=F32)


def _proj_residual(a, w, res, *, tm=512, tn=512):
    T, K = a.shape
    N = w.shape[1]
    return pl.pallas_call(
        _proj_res_kernel,
        grid=(T // tm, N // tn),
        in_specs=[pl.BlockSpec((tm, K), lambda i, j: (i, 0)),
                  pl.BlockSpec((K, tn), lambda i, j: (0, j)),
                  pl.BlockSpec((tm, tn), lambda i, j: (i, j))],
        out_specs=pl.BlockSpec((tm, tn), lambda i, j: (i, j)),
        out_shape=jax.ShapeDtypeStruct((T, N), F32),
        compiler_params=_cparams(("parallel", "arbitrary")),
        name="proj_residual",
    )(a, w, res)


def _ffn_kernel(h_ref, g_ref, wg_ref, wu_ref, wd_ref, o_ref, hn_ref):
    f = pl.program_id(1)

    @pl.when(f == 0)
    def _():
        x = h_ref[...]
        r = lax.rsqrt(jnp.mean(x * x, axis=-1, keepdims=True) + NORM_EPS)
        hn_ref[...] = (x * r * g_ref[...]).astype(BF16)
        o_ref[...] = x

    hn = hn_ref[...]
    gate = jnp.dot(hn, wg_ref[...], preferred_element_type=F32)
    up = jnp.dot(hn, wu_ref[...], preferred_element_type=F32)
    act = (gate * jax.nn.sigmoid(gate) * up).astype(BF16)
    o_ref[...] += jnp.dot(act, wd_ref[...], preferred_element_type=F32)


def _dense_ffn(h2, norm_gain, wg, wu, wd, *, tm=1024, tf=512):
    T, D = h2.shape
    F = wg.shape[1]
    return pl.pallas_call(
        _ffn_kernel,
        grid=(T // tm, F // tf),
        in_specs=[pl.BlockSpec((tm, D), lambda i, f: (i, 0)),
                  pl.BlockSpec((1, D), lambda i, f: (0, 0)),
                  pl.BlockSpec((D, tf), lambda i, f: (0, f)),
                  pl.BlockSpec((D, tf), lambda i, f: (0, f)),
                  pl.BlockSpec((tf, D), lambda i, f: (f, 0))],
        out_specs=pl.BlockSpec((tm, D), lambda i, f: (i, 0)),
        out_shape=jax.ShapeDtypeStruct((T, D), F32),
        scratch_shapes=[pltpu.VMEM((tm, D), BF16)],
        compiler_params=_cparams(("parallel", "arbitrary")),
        name="dense_ffn",
    )(h2, norm_gain.reshape(1, D), wg, wu, wd)


def _router_kernel(h_ref, g_ref, wr_ref, br_ref, gate_ref):
    x = h_ref[...]
    r = lax.rsqrt(jnp.mean(x * x, axis=-1, keepdims=True) + NORM_EPS)
    hn = x * r * g_ref[...]
    logits = jnp.dot(hn, wr_ref[...], preferred_element_type=F32,
                     precision=lax.Precision.HIGHEST) + br_ref[...]
    lane = lax.broadcasted_iota(jnp.int32, logits.shape, 1)
    logits = jnp.where(lane < N_EXPERTS, logits, NEG_INF)
    m1 = jnp.max(logits, axis=-1, keepdims=True)
    i1 = jnp.min(jnp.where(logits == m1, lane, LANES), axis=-1, keepdims=True)
    rest = jnp.where(lane == i1, NEG_INF, logits)
    m2 = jnp.max(rest, axis=-1, keepdims=True)
    i2 = jnp.min(jnp.where(rest == m2, lane, LANES), axis=-1, keepdims=True)
    e2 = jnp.exp(m2 - m1)
    w1 = 1.0 / (1.0 + e2)
    w2 = e2 / (1.0 + e2)
    gate_ref[...] = jnp.where(lane == i1, w1, 0.0) + jnp.where(lane == i2, w2, 0.0)


def _router(h2, norm_gain, w_router, b_router, *, tm=512):
    T, D = h2.shape
    wr = jnp.zeros((D, LANES), F32).at[:, :N_EXPERTS].set(w_router)
    br = jnp.zeros((1, LANES), F32).at[0, :N_EXPERTS].set(b_router)
    return pl.pallas_call(
        _router_kernel,
        grid=(T // tm,),
        in_specs=[pl.BlockSpec((tm, D), lambda i: (i, 0)),
                  pl.BlockSpec((1, D), lambda i: (0, 0)),
                  pl.BlockSpec((D, LANES), lambda i: (0, 0)),
                  pl.BlockSpec((1, LANES), lambda i: (0, 0))],
        out_specs=pl.BlockSpec((tm, LANES), lambda i: (i, 0)),
        out_shape=jax.ShapeDtypeStruct((T, LANES), F32),
        compiler_params=_cparams(("parallel",)),
        name="moe_router",
    )(h2, norm_gain.reshape(1, D), wr, br)


def _moe_dense_kernel(h_ref, g_ref, gate_ref, wg_ref, wu_ref, wd_ref, o_ref, hn_ref):
    e = pl.program_id(1)
    f = pl.program_id(2)

    @pl.when((e == 0) & (f == 0))
    def _():
        x = h_ref[...]
        r = lax.rsqrt(jnp.mean(x * x, axis=-1, keepdims=True) + NORM_EPS)
        hn_ref[...] = (x * r * g_ref[...]).astype(BF16)
        o_ref[...] = x

    lane = lax.broadcasted_iota(jnp.int32, gate_ref.shape, 1)
    gcol = jnp.sum(jnp.where(lane == e, gate_ref[...], 0.0), axis=-1, keepdims=True)
    hn = hn_ref[...]
    gate = jnp.dot(hn, wg_ref[0], preferred_element_type=F32)
    up = jnp.dot(hn, wu_ref[0], preferred_element_type=F32)
    act = (gate * jax.nn.sigmoid(gate) * up).astype(BF16)
    o_ref[...] += gcol * jnp.dot(act, wd_ref[0], preferred_element_type=F32)


def _moe_ffn(h2, norm_gain, w_router, b_router, we_gate, we_up, we_down, *, tm=1024, tf=512):
    T, D = h2.shape
    E, _, F = we_gate.shape
    gates = _router(h2, norm_gain, w_router, b_router)
    return pl.pallas_call(
        _moe_dense_kernel,
        grid=(T // tm, E, F // tf),
        in_specs=[pl.BlockSpec((tm, D), lambda i, e, f: (i, 0)),
                  pl.BlockSpec((1, D), lambda i, e, f: (0, 0)),
                  pl.BlockSpec((tm, LANES), lambda i, e, f: (i, 0)),
                  pl.BlockSpec((1, D, tf), lambda i, e, f: (e, 0, f)),
                  pl.BlockSpec((1, D, tf), lambda i, e, f: (e, 0, f)),
                  pl.BlockSpec((1, tf, D), lambda i, e, f: (e, f, 0))],
        out_specs=pl.BlockSpec((tm, D), lambda i, e, f: (i, 0)),
        out_shape=jax.ShapeDtypeStruct((T, D), F32),
        scratch_shapes=[pltpu.VMEM((tm, D), BF16)],
        compiler_params=_cparams(("parallel", "arbitrary", "arbitrary")),
        name="moe_experts",
    )(h2, norm_gain.reshape(1, D), gates, we_gate, we_up, we_down)


def _rel_bucket(dist):
    n = jnp.maximum(dist, 0)
    max_exact = REL_BUCKETS // 2
    nf = jnp.maximum(n, max_exact).astype(F32)
    large = max_exact + (jnp.log(nf / max_exact) / math.log(REL_MAX_DIST / max_exact)
                         * (REL_BUCKETS - max_exact)).astype(jnp.int32)
    return jnp.where(n < max_exact, n, jnp.minimum(large, REL_BUCKETS - 1))


def _bias_tiles_kernel(tab_ref, idx_ref, o_ref):
    t = idx_ref.shape[-1]

    def chunk(c, carry):
        rows = pl.ds(pl.multiple_of(c * 8, 8), 8)
        idx = idx_ref[0, rows, :]
        for h in range(N_SCORE_HEADS):
            tile = jnp.zeros((8, t), F32)
            for b in range(REL_BUCKETS):
                tile = jnp.where(idx == b, tab_ref[h, b] * LOG2E, tile)
            o_ref[h, 0, rows, :] = tile
        return carry

    lax.fori_loop(0, t // 8, chunk, 0)


def _bias_tiles(rel_bias, S, t):
    n_delta = S // t
    i = jnp.arange(t)
    dist = (jnp.arange(n_delta) * t)[:, None, None] + i[None, None, :] - i[None, :, None]
    idx = _rel_bucket(dist).astype(jnp.int32)
    return pl.pallas_call(
        _bias_tiles_kernel,
        grid=(n_delta,),
        in_specs=[pl.BlockSpec(memory_space=pltpu.SMEM),
                  pl.BlockSpec((1, t, t), lambda d: (d, 0, 0))],
        out_specs=pl.BlockSpec((N_SCORE_HEADS, 1, t, t), lambda d: (0, d, 0, 0)),
        out_shape=jax.ShapeDtypeStruct((N_SCORE_HEADS, n_delta, t, t), F32),
        compiler_params=_cparams(("parallel",)),
        name="bias_tiles",
    )(rel_bias.T, idx)


def _flash_steps(mls, acc_refs, scores, masks, v_ts):
    new_mls, alphas, probs = [], [], []
    for (m, l), s, mask in zip(mls, scores, masks):
        if mask is not None:
            s = jnp.where(mask, s, NEG_INF)
        m_new = jnp.maximum(m, jnp.max(s, axis=0, keepdims=True))
        alpha = jnp.exp2(m - m_new)
        p = jnp.exp2(s - m_new)
        if mask is not None:
            p = jnp.where(mask, p, 0.0)
        new_mls.append((m_new, alpha * l + jnp.sum(p, axis=0, keepdims=True)))
        alphas.append(alpha)
        probs.append(p.astype(BF16))
    for acc_ref, alpha, p, v_t in zip(acc_refs, alphas, probs, v_ts):
        acc_ref[...] = alpha * acc_ref[...] + jnp.dot(v_t, p, preferred_element_type=F32)
    return tuple(new_mls)


def _flash_init(nq):
    return (jnp.full((1, nq), NEG_INF, F32), jnp.zeros((1, nq), F32))


def _flash_out(ml, acc_ref):
    return acc_ref[...] / jnp.maximum(ml[1], 1e-30)


def _key_tile(kb, t):
    return pl.ds(pl.multiple_of(kb * t, t), t)


def _causal_t(t):
    return lax.broadcasted_iota(jnp.int32, (t, t), 1) >= lax.broadcasted_iota(jnp.int32, (t, t), 0)


def _row_band(x, lo, hi):
    row = lax.broadcasted_iota(jnp.int32, (x.shape[0], 1), 0)
    return jnp.where((row >= lo) & (row < hi), x, jnp.zeros_like(x))


def _rank_select(score, cand_valid, topk):
    C = score.shape[0]
    row = lax.broadcasted_iota(jnp.int32, score.shape, 0)
    rank = jnp.zeros(score.shape, F32)
    for c in range(C):
        sc = score[c:c + 1, :]
        tie = (row > c).astype(F32)
        rank = rank + jnp.where(sc > score, 1.0, jnp.where(sc == score, tie, 0.0))
    return jnp.where(cand_valid & (rank < topk), 1.0, 0.0)


def _diff_attn_kernel(lam_ref, sub_ref, qt_ref, k_ref, vt_ref, bias_ref, o_ref, acc_ref, *, lambda_init):
    t = ATT_TILE
    qb = pl.program_id(2)
    qt = qt_ref[0]
    qm = [_row_band(qt, mi * HEAD_DIM, (mi + 1) * HEAD_DIM) for mi in range(2)]
    acc_ref[...] = jnp.zeros(acc_ref.shape, F32)

    def step(kb, mls, mask):
        k = k_ref[0, _key_tile(kb, t), :]
        v_t = vt_ref[0, :, _key_tile(kb, t)]
        scores = [jnp.dot(k, qm[mi], preferred_element_type=F32) + bias_ref[mi, qb - kb] for mi in range(2)]
        return _flash_steps(mls, [acc_ref.at[mi] for mi in range(2)], scores, [mask] * 2, [v_t] * 2)

    mls = lax.fori_loop(0, qb, lambda kb, mls: step(kb, mls, None), (_flash_init(t), _flash_init(t)))
    mls = step(qb, mls, _causal_t(t))
    lv = lam_ref[...]
    lam = (jnp.exp(jnp.sum(lv[0:1] * lv[1:2], axis=-1, keepdims=True))
           - jnp.exp(jnp.sum(lv[2:3] * lv[3:4], axis=-1, keepdims=True)) + lambda_init)
    o = _flash_out(mls[0], acc_ref.at[0]) - lam * _flash_out(mls[1], acc_ref.at[1])
    r = lax.rsqrt(jnp.mean(o * o, axis=0, keepdims=True) + NORM_EPS)
    o_ref[0] = (o * r * sub_ref[...] * (1.0 - lambda_init)).astype(o_ref.dtype)


def _diff_attention(q_t, qkv, v_t, lam4, subln_gain, bias_tiles, lambda_init):
    B, S, _ = qkv.shape
    t = ATT_TILE
    nq = S // t
    H = DIFF_HEADS
    return pl.pallas_call(
        functools.partial(_diff_attn_kernel, lambda_init=lambda_init),
        grid=(H, B, nq),
        in_specs=[pl.BlockSpec((4, HEAD_DIM), lambda h, b, i: (0, 0)),
                  pl.BlockSpec((LANES, 1), lambda h, b, i: (0, 0)),
                  pl.BlockSpec((1, LANES, t), lambda h, b, i: (b, h, i)),
                  pl.BlockSpec((1, S, LANES), lambda h, b, i: (b, 0, H + h)),
                  pl.BlockSpec((1, LANES, S), lambda h, b, i: (b, h, 0)),
                  pl.BlockSpec((2, nq, t, t), lambda h, b, i: (h, 0, 0, 0))],
        out_specs=pl.BlockSpec((1, LANES, t), lambda h, b, i: (b, h, i)),
        out_shape=jax.ShapeDtypeStruct((B, H * LANES, S), BF16),
        scratch_shapes=[pltpu.VMEM((2, LANES, t), F32)],
        compiler_params=_cparams(("parallel", "parallel", "arbitrary")),
        name="diff_attention",
    )(lam4, subln_gain.reshape(LANES, 1), q_t, qkv, v_t, bias_tiles)


def _moba_kernel(qt_ref, k_ref, vt_ref, bias_ref, o_ref, kmean_ref, sel_ref, acc_ref):
    t = ATT_TILE
    qb = pl.program_id(2)
    n_blk = kmean_ref.shape[0]

    @pl.when(qb == 0)
    def _():
        for n in range(n_blk):
            kb = k_ref[0, n * t:(n + 1) * t, :].astype(F32)
            kmean_ref[n:n + 1, :] = jnp.mean(kb, axis=0, keepdims=True)

    qt = qt_ref[0]
    qm = [_row_band(qt, hj * HEAD_DIM, (hj + 1) * HEAD_DIM) for hj in range(2)]
    km = kmean_ref[...]
    km_hi = km.astype(BF16)
    km_lo = (km - km_hi.astype(F32)).astype(BF16)
    past = lax.broadcasted_iota(jnp.int32, (n_blk, t), 0) < qb
    for hj in range(2):
        gate = (jnp.dot(km_hi, qm[hj], preferred_element_type=F32)
                + jnp.dot(km_lo, qm[hj], preferred_element_type=F32))
        sel_ref[hj] = _rank_select(jnp.where(past, gate, NEG_INF), past, MOBA_TOPK)
    acc_ref[...] = jnp.zeros(acc_ref.shape, F32)

    def step(kb, mls, diagonal):
        k = k_ref[0, _key_tile(kb, t), :]
        v_t = vt_ref[0, :, _key_tile(kb, t)]
        scores = [jnp.dot(k, qm[hj], preferred_element_type=F32) + bias_ref[hj, qb - kb] for hj in range(2)]
        masks = [_causal_t(t) if diagonal else sel_ref[hj, pl.ds(kb, 1), :] > 0.5 for hj in range(2)]
        return _flash_steps(mls, [acc_ref.at[hj] for hj in range(2)], scores, masks,
                            [v_t[hj * HEAD_DIM:(hj + 1) * HEAD_DIM, :] for hj in range(2)])

    mls = lax.fori_loop(0, qb, lambda kb, mls: step(kb, mls, False), (_flash_init(t), _flash_init(t)))
    mls = step(qb, mls, True)
    for hj in range(2):
        o_ref[0, hj * HEAD_DIM:(hj + 1) * HEAD_DIM, :] = _flash_out(mls[hj], acc_ref.at[hj]).astype(o_ref.dtype)


def _moba_attention(q_t, qkv, v_t, bias_tiles):
    B, S, _ = qkv.shape
    t = ATT_TILE
    assert t == MOBA_BLOCK and S % t == 0
    nq = S // t
    HP = D_MODEL // LANES
    return pl.pallas_call(
        _moba_kernel,
        grid=(HP, B, nq),
        in_specs=[pl.BlockSpec((1, LANES, t), lambda h, b, i: (b, h, i)),
                  pl.BlockSpec((1, S, LANES), lambda h, b, i: (b, 0, HP + h)),
                  pl.BlockSpec((1, LANES, S), lambda h, b, i: (b, h, 0)),
                  pl.BlockSpec((2, nq, t, t), lambda h, b, i: (h, 0, 0, 0))],
        out_specs=pl.BlockSpec((1, LANES, t), lambda h, b, i: (b, h, i)),
        out_shape=jax.ShapeDtypeStruct((B, D_MODEL, S), BF16),
        scratch_shapes=[pltpu.VMEM((S // t, LANES), F32),
                        pltpu.VMEM((2, S // t, t), F32),
                        pltpu.VMEM((2, HEAD_DIM, t), F32)],
        compiler_params=_cparams(("parallel", "parallel", "arbitrary")),
        name="moba_attention",
    )(q_t, qkv, v_t, bias_tiles)


def _compress_kernel(x_ref, pos_ref, w1_ref, w2_ref, gain_ref, o_ref, *, normalize):
    G, n_half, width = x_ref.shape[1:]
    x = x_ref[0].reshape(G * n_half, width).astype(F32)
    first = (x + pos_ref[0:1, :]).astype(BF16)
    second = (x + pos_ref[1:2, :]).astype(BF16)
    u = jnp.dot(first, w1_ref[0:width, :], preferred_element_type=F32)
    low = jnp.dot(second, w1_ref[width:2 * width, :], preferred_element_type=F32)
    rows = G * n_half
    hid = u + pltpu.roll(low, rows - 1, 0)
    c = jnp.dot(jax.nn.gelu(hid).astype(BF16), w2_ref[...], preferred_element_type=F32)
    if normalize:
        r = lax.rsqrt(jnp.mean(c * c, axis=-1, keepdims=True) + NORM_EPS)
        c = c * r * gain_ref[...]
    o_ref[0] = c.reshape(G, n_half, HEAD_DIM).astype(o_ref.dtype)


def _compress(x, pos, w1, w2, gain, normalize):
    B, G, S, d = x.shape
    n_half = S // NSA_CMP_STRIDE
    width = NSA_CMP_STRIDE * d
    return pl.pallas_call(
        functools.partial(_compress_kernel, normalize=normalize),
        grid=(B,),
        in_specs=[pl.BlockSpec((1, G, n_half, width), lambda b: (b, 0, 0, 0)),
                  pl.BlockSpec((2, width), lambda b: (0, 0)),
                  pl.BlockSpec((2 * width, NSA_CMP_HIDDEN), lambda b: (0, 0)),
                  pl.BlockSpec((NSA_CMP_HIDDEN, d), lambda b: (0, 0)),
                  pl.BlockSpec((1, d), lambda b: (0, 0))],
        out_specs=pl.BlockSpec((1, G, n_half, d), lambda b: (b, 0, 0, 0)),
        out_shape=jax.ShapeDtypeStruct((B, G, n_half, d), BF16),
        compiler_params=_cparams(("parallel",)),
        name="nsa_compress",
    )(x.reshape(B, G, n_half, width), pos.reshape(2, width), w1, w2, gain.reshape(1, d))


def _nsa_kernel(qt_ref, kc_ref, vct_ref, ks_ref, vst_ref, kw_ref, vwt_ref, g_ref, bias_ref,
                ov_ref, o_ref, sel_ref, acc_ref, ocmp_ref, osel_ref):
    t = ATT_TILE
    R = NSA_GROUP_SIZE
    d = HEAD_DIM
    qb = pl.program_id(2)
    n_cmp_rows = kc_ref.shape[2]
    n_sel = ov_ref.shape[0]
    qt = qt_ref[0]
    qh = [qt[r * d:(r + 1) * d, :] for r in range(R)]
    qpos = qb * t + lax.broadcasted_iota(jnp.int32, (1, t), 1)

    n_idx = lax.broadcasted_iota(jnp.int32, (n_cmp_rows, 1), 0)
    cmask = (n_idx * NSA_CMP_STRIDE + (NSA_CMP_BLOCK - 1) <= qpos) & (n_idx < n_cmp_rows - 1)
    kc = kc_ref[0, 0]
    psum = jnp.zeros((n_cmp_rows, t), F32)
    for r in range(R):
        cl = jnp.where(cmask, jnp.dot(kc, qh[r], preferred_element_type=F32), NEG_INF)
        cm = jnp.max(cl, axis=0, keepdims=True)
        cp = jnp.where(cmask, jnp.exp2(cl - cm), 0.0)
        cp = cp / jnp.maximum(jnp.sum(cp, axis=0, keepdims=True), 1e-30)
        ocmp_ref[r] = jnp.dot(vct_ref[0, 0], cp.astype(BF16), preferred_element_type=F32)
        psum = psum + cp

    p_hi = psum.astype(BF16)
    p_lo = (psum - p_hi.astype(F32)).astype(BF16)
    imp = (jnp.dot(ov_ref[...], p_hi, preferred_element_type=F32)
           + jnp.dot(ov_ref[...], p_lo, preferred_element_type=F32))
    blk = lax.broadcasted_iota(jnp.int32, (n_sel, 1), 0)
    cur = qpos // NSA_SEL_BLOCK
    forced = (blk == 0) | (blk == cur) | (blk == cur - 1)
    score = jnp.where(forced, FORCE_SCORE, jnp.where(blk <= cur, imp, NEG_INF))
    sel_ref[...] = _rank_select(score, blk <= cur, NSA_SEL_TOPK)

    causal = _causal_t(t)
    per_tile = t // NSA_SEL_BLOCK

    def sweep(k_ref, vt_ref, kb, delta, mls, mask):
        k = k_ref[0, 0, _key_tile(kb, t), :]
        v_t = vt_ref[0, :, _key_tile(kb, t)]
        scores = [jnp.dot(k, qh[r], preferred_element_type=F32) + bias_ref[r, delta] for r in range(R)]
        return _flash_steps(mls, [acc_ref.at[r] for r in range(R)], scores, [mask] * R, [v_t] * R)

    def chosen_rows(kb):
        rows = [jnp.broadcast_to(sel_ref[pl.ds(kb * per_tile + c, 1), :], (NSA_SEL_BLOCK, t))
                for c in range(per_tile)]
        return jnp.concatenate(rows, axis=0) > 0.5

    init = tuple(_flash_init(t) for _ in range(R))

    acc_ref[...] = jnp.zeros(acc_ref.shape, F32)
    mls = lax.fori_loop(0, qb, lambda kb, mls: sweep(ks_ref, vst_ref, kb, qb - kb, mls, chosen_rows(kb)), init)
    mls = sweep(ks_ref, vst_ref, qb, 0, mls, chosen_rows(qb) & causal)
    for r in range(R):
        osel_ref[r] = _flash_out(mls[r], acc_ref.at[r])

    acc_ref[...] = jnp.zeros(acc_ref.shape, F32)
    key_j = lax.broadcasted_iota(jnp.int32, (t, t), 0)
    qry_i = lax.broadcasted_iota(jnp.int32, (t, t), 1)
    mls = init
    for back in range(NSA_WINDOW // t, -1, -1):
        dist = back * t + qry_i - key_j
        wmask = (dist >= 0) & (dist < NSA_WINDOW) & (qb >= back)
        mls = sweep(kw_ref, vwt_ref, jnp.maximum(qb - back, 0), back, mls, wmask)

    gates = jax.nn.sigmoid(g_ref[0, 0].astype(F32))
    for r in range(R):
        o = (gates[3 * r:3 * r + 1, :] * ocmp_ref[r]
             + gates[3 * r + 1:3 * r + 2, :] * osel_ref[r]
             + gates[3 * r + 2:3 * r + 3, :] * _flash_out(mls[r], acc_ref.at[r]))
        o_ref[0, r * d:(r + 1) * d, :] = o.astype(o_ref.dtype)


def _nsa_overlap_t(S):
    n_cmp_rows = S // NSA_CMP_STRIDE
    n_sel = S // NSA_SEL_BLOCK
    cmp_start = jnp.arange(n_cmp_rows) * NSA_CMP_STRIDE
    sel_start = jnp.arange(n_sel) * NSA_SEL_BLOCK
    overlap = jnp.clip(jnp.minimum(cmp_start[None, :] + NSA_CMP_BLOCK, sel_start[:, None] + NSA_SEL_BLOCK)
                       - jnp.maximum(cmp_start[None, :], sel_start[:, None]), 0).astype(F32) / NSA_CMP_STRIDE
    return jnp.where(cmp_start[None, :] + NSA_CMP_BLOCK <= S, overlap, 0.0).astype(BF16)


def _nsa_attention(q_t, kc, vc_t, ks, vs_t, kw, vw_t, gates_t, bias_tiles):
    B, _, S = q_t.shape
    t = ATT_TILE
    nq = S // t
    G, R, d = NSA_KV_GROUPS, NSA_GROUP_SIZE, HEAD_DIM
    n_cmp_rows = kc.shape[2]
    overlap_t = _nsa_overlap_t(S)
    n_sel = overlap_t.shape[0]
    k_spec = pl.BlockSpec((1, 1, S, d), lambda g, b, i: (b, g, 0, 0))
    vt_spec = pl.BlockSpec((1, d, S), lambda g, b, i: (b, g, 0))
    return pl.pallas_call(
        _nsa_kernel,
        grid=(G, B, nq),
        in_specs=[pl.BlockSpec((1, R * d, t), lambda g, b, i: (b, g, i)),
                  pl.BlockSpec((1, 1, n_cmp_rows, d), lambda g, b, i: (b, g, 0, 0)),
                  pl.BlockSpec((1, 1, d, n_cmp_rows), lambda g, b, i: (b, g, 0, 0)),
                  k_spec, vt_spec, k_spec, vt_spec,
                  pl.BlockSpec((1, 1, 3 * R, t), lambda g, b, i: (b, g, 0, i)),
                  pl.BlockSpec((R, nq, t, t), lambda g, b, i: (g, 0, 0, 0)),
                  pl.BlockSpec((n_sel, n_cmp_rows), lambda g, b, i: (0, 0))],
        out_specs=pl.BlockSpec((1, R * d, t), lambda g, b, i: (b, g, i)),
        out_shape=jax.ShapeDtypeStruct((B, G * R * d, S), BF16),
        scratch_shapes=[pltpu.VMEM((n_sel, t), F32),
                        pltpu.VMEM((R, d, t), F32),
                        pltpu.VMEM((R, d, t), F32),
                        pltpu.VMEM((R, d, t), F32)],
        compiler_params=_cparams(("parallel", "parallel", "arbitrary")),
        name="nsa_attention",
    )(q_t, kc, vc_t, ks, vs_t, kw, vw_t, gates_t, bias_tiles, overlap_t)


def _qkv_head_gain(q_gain, k_gain, n_heads, n_plain):
    return jnp.concatenate([jnp.tile(q_gain, n_heads) * Q_SCALE, jnp.tile(k_gain, n_heads),
                            jnp.ones((n_plain,), F32)])


def _swap_last(x):
    return x.transpose(0, 2, 1)


def _diff_layer(h2, B, S, norm_mix, p, bias_tiles, lambda_init):
    w_in, w_out, q_gain, k_gain, lq1, lk1, lq2, lk2, subln = p
    hg = _qkv_head_gain(q_gain, k_gain, 2 * DIFF_HEADS, D_MODEL)
    qkv = _norm_proj(h2, norm_mix, w_in.astype(BF16), hg, 2 * D_MODEL).reshape(B, S, -1)
    o_t = _diff_attention(_swap_last(qkv[:, :, :D_MODEL]), qkv, _swap_last(qkv[:, :, 2 * D_MODEL:]),
                          jnp.stack([lq1, lk1, lq2, lk2]), subln, bias_tiles, lambda_init)
    return _proj_residual(_swap_last(o_t).reshape(B * S, -1), w_out.astype(BF16), h2)


def _moba_layer(h2, B, S, norm_mix, p, bias_tiles):
    w_in, w_out, q_gain, k_gain = p
    hg = _qkv_head_gain(q_gain, k_gain, N_SCORE_HEADS, D_MODEL)
    qkv = _norm_proj(h2, norm_mix, w_in.astype(BF16), hg, 2 * D_MODEL).reshape(B, S, -1)
    o_t = _moba_attention(_swap_last(qkv[:, :, :D_MODEL]), qkv, _swap_last(qkv[:, :, 2 * D_MODEL:]), bias_tiles)
    return _proj_residual(_swap_last(o_t).reshape(B * S, -1), w_out.astype(BF16), h2)


def _nsa_layer(h2, B, S, norm_mix, p, bias_tiles):
    w_in, w_out, q_gain, k_gain, pos_k, pos_v, k_w1, k_w2, v_w1, v_w2 = p
    G, d = NSA_KV_GROUPS, HEAD_DIM
    kvw = G * d
    qw = N_SCORE_HEADS * d
    sec = {name: slice(qw + i * kvw, qw + (i + 1) * kvw) for i, name in
           enumerate(("kc", "vc", "ks", "vs", "kw", "vw"))}
    n_gate = 3 * N_SCORE_HEADS
    tn = 384
    n_used = qw + 6 * kvw + n_gate
    n_pad = -n_used % tn
    w_perm = jnp.concatenate([w_in[:, :qw], w_in[:, sec["ks"]], w_in[:, sec["kw"]], w_in[:, sec["kc"]],
                              w_in[:, sec["vc"]], w_in[:, sec["vs"]], w_in[:, sec["vw"]],
                              w_in[:, qw + 6 * kvw:], jnp.zeros((D_MODEL, n_pad), F32)], axis=1)
    hg = jnp.concatenate([jnp.tile(q_gain, N_SCORE_HEADS) * Q_SCALE, jnp.tile(k_gain[1], G),
                          jnp.tile(k_gain[2], G), jnp.ones((n_used + n_pad - qw - 2 * kvw,), F32)])
    proj = _norm_proj(h2, norm_mix, w_perm.astype(BF16), hg, qw + 2 * kvw, tn=tn).reshape(B, S, -1)

    def section(i):
        return proj[:, :, qw + i * kvw:qw + (i + 1) * kvw]

    def group_major(x):
        return x.reshape(B, S, G, d).transpose(0, 2, 1, 3)

    ks, kw, kc, vc, vs, vw = (section(i) for i in range(6))
    gates_t = _swap_last(proj[:, :, qw + 6 * kvw:n_used]).reshape(B, G, n_gate // G, S)
    k_cmp = _compress(group_major(kc), pos_k, k_w1.astype(BF16), k_w2.astype(BF16), k_gain[0], True)
    v_cmp = _compress(group_major(vc), pos_v, v_w1.astype(BF16), v_w2.astype(BF16), k_gain[0], False)
    o_t = _nsa_attention(_swap_last(proj[:, :, :qw]), k_cmp, v_cmp.transpose(0, 1, 3, 2),
                         group_major(ks), _swap_last(vs), group_major(kw), _swap_last(vw), gates_t, bias_tiles)
    return _proj_residual(_swap_last(o_t).reshape(B * S, qw), w_out.astype(BF16), h2)


def _diff_lambda_init(layer):
    return 0.8 - 0.6 * math.exp(-0.3 * layer)


def kernel(x, rel_bias,
           l0_norm_mix, l0_w_in, l0_w_out, l0_q_gain, l0_k_gain, l0_lam_q1, l0_lam_k1, l0_lam_q2, l0_lam_k2,
           l0_subln_gain, l0_norm_ffn, l0_w_gate, l0_w_up, l0_w_down,
           l1_norm_mix, l1_w_in, l1_w_out, l1_q_gain, l1_k_gain, l1_norm_ffn, l1_w_router, l1_b_router,
           l1_we_gate, l1_we_up, l1_we_down,
           l2_norm_mix, l2_w_in, l2_w_out, l2_q_gain, l2_k_gain, l2_cmp_pos_k, l2_cmp_pos_v, l2_cmp_k_w1,
           l2_cmp_k_w2, l2_cmp_v_w1, l2_cmp_v_w2, l2_norm_ffn, l2_w_gate, l2_w_up, l2_w_down,
           l3_norm_mix, l3_w_in, l3_w_out, l3_q_gain, l3_k_gain, l3_lam_q1, l3_lam_k1, l3_lam_q2, l3_lam_k2,
           l3_subln_gain, l3_norm_ffn, l3_w_router, l3_b_router, l3_we_gate, l3_we_up, l3_we_down):
    B, S, D = x.shape
    bias_tiles = _bias_tiles(rel_bias, S, ATT_TILE)
    h = x.reshape(B * S, D)

    h = _diff_layer(h, B, S, l0_norm_mix, (l0_w_in, l0_w_out, l0_q_gain, l0_k_gain, l0_lam_q1, l0_lam_k1,
                                           l0_lam_q2, l0_lam_k2, l0_subln_gain), bias_tiles, _diff_lambda_init(0))
    h = _dense_ffn(h, l0_norm_ffn, l0_w_gate.astype(BF16), l0_w_up.astype(BF16), l0_w_down.astype(BF16))

    h = _moba_layer(h, B, S, l1_norm_mix, (l1_w_in, l1_w_out, l1_q_gain, l1_k_gain), bias_tiles)
    h = _moe_ffn(h, l1_norm_ffn, l1_w_router, l1_b_router, l1_we_gate.astype(BF16), l1_we_up.astype(BF16),
                 l1_we_down.astype(BF16))

    h = _nsa_layer(h, B, S, l2_norm_mix, (l2_w_in, l2_w_out, l2_q_gain, l2_k_gain, l2_cmp_pos_k, l2_cmp_pos_v,
                                          l2_cmp_k_w1, l2_cmp_k_w2, l2_cmp_v_w1, l2_cmp_v_w2), bias_tiles)
    h = _dense_ffn(h, l2_norm_ffn, l2_w_gate.astype(BF16), l2_w_up.astype(BF16), l2_w_down.astype(BF16))

    h = _diff_layer(h, B, S, l3_norm_mix, (l3_w_in, l3_w_out, l3_q_gain, l3_k_gain, l3_lam_q1, l3_lam_k1,
                                           l3_lam_q2, l3_lam_k2, l3_subln_gain), bias_tiles, _diff_lambda_init(3))
    h = _moe_ffn(h, l3_norm_ffn, l3_w_router, l3_b_router, l3_we_gate.astype(BF16), l3_we_up.astype(BF16),
                 l3_we_down.astype(BF16))
    return h.reshape(B, S, D)
```

```python
import functools
import math

import jax
import jax.numpy as jnp
from jax import lax
from jax.experimental import pallas as pl
from jax.experimental.pallas import tpu as pltpu

F32 = jnp.float32
BF16 = jnp.bfloat16

D_MODEL = 1024
HEAD_DIM = 64
N_SCORE_HEADS = 16
DIFF_HEADS = 8
MOBA_BLOCK = 256
MOBA_TOPK = 3
NSA_KV_GROUPS = 4
NSA_GROUP_SIZE = 4
NSA_CMP_BLOCK = 32
NSA_CMP_STRIDE = 16
NSA_CMP_HIDDEN = 256
NSA_SEL_BLOCK = 64
NSA_SEL_TOPK = 16
NSA_WINDOW = 512
REL_BUCKETS = 32
REL_MAX_DIST = 1024
FFN_DIM = 3584
N_EXPERTS = 8
NORM_EPS = 1e-6
NEG_INF = -1e30
FORCE_SCORE = 1e30
ATTN_SCALE = HEAD_DIM ** -0.5
LOG2E = math.log2(math.e)
Q_SCALE = ATTN_SCALE * LOG2E

LANES = 128
ATT_TILE = 256
VMEM_LIMIT = 56 * 1024 * 1024


def _cparams(sem):
    return pltpu.CompilerParams(dimension_semantics=sem, vmem_limit_bytes=VMEM_LIMIT)


def _dot_nt(a, b):
    return lax.dot_general(a, b, (((1,), (1,)), ((), ())), preferred_element_type=F32)


def _norm_proj_kernel(x_ref, g_ref, w_ref, hg_ref, o_ref, xn_ref, *, n_norm_tiles, tn):
    j = pl.program_id(1)

    @pl.when(j == 0)
    def _():
        x = x_ref[...]
        r = lax.rsqrt(jnp.mean(x * x, axis=-1, keepdims=True) + NORM_EPS)
        xn_ref[...] = (x * r * g_ref[...]).astype(BF16)

    acc = jnp.dot(xn_ref[...], w_ref[...], preferred_element_type=F32)

    @pl.when(j < n_norm_tiles)
    def _():
        for s in range(tn // HEAD_DIM):
            sl = slice(s * HEAD_DIM, (s + 1) * HEAD_DIM)
            seg = acc[:, sl]
            r = lax.rsqrt(jnp.mean(seg * seg, axis=-1, keepdims=True) + NORM_EPS)
            o_ref[:, sl] = (seg * r * hg_ref[:, sl]).astype(o_ref.dtype)

    @pl.when(j >= n_norm_tiles)
    def _():
        o_ref[...] = acc.astype(o_ref.dtype)


def _norm_proj(h2, norm_gain, w, head_gain, n_norm_cols, *, tm=512, tn=512):
    T, D = h2.shape
    N = w.shape[1]
    assert T % tm == 0 and N % tn == 0 and n_norm_cols % tn == 0
    return pl.pallas_call(
        functools.partial(_norm_proj_kernel, n_norm_tiles=n_norm_cols // tn, tn=tn),
        grid=(T // tm, N // tn),
        in_specs=[pl.BlockSpec((tm, D), lambda i, j: (i, 0)),
                  pl.BlockSpec((1, D), lambda i, j: (0, 0)),
                  pl.BlockSpec((D, tn), lambda i, j: (0, j)),
                  pl.BlockSpec((1, tn), lambda i, j: (0, j))],
        out_specs=pl.BlockSpec((tm, tn), lambda i, j: (i, j)),
        out_shape=jax.ShapeDtypeStruct((T, N), BF16),
        scratch_shapes=[pltpu.VMEM((tm, D), BF16)],
        compiler_params=_cparams(("parallel", "arbitrary")),
        name="norm_proj",
    )(h2, norm_gain.reshape(1, D), w, head_gain.reshape(1, N))


def _proj_res_kernel(a_ref, w_ref, r_ref, o_ref):
    o_ref[...] = r_ref[...] + jnp.dot(a_ref[...], w_ref[...], preferred_element_type=F32)


def _proj_residual(a, w, res, *, tm=512, tn=512):
    T, K = a.shape
    N = w.shape[1]
    return pl.pallas_call(
        _proj_res_kernel,
        grid=(T // tm, N // tn),
        in_specs=[pl.BlockSpec((tm, K), lambda i, j: (i, 0)),
                  pl.BlockSpec((K, tn), lambda i, j: (0, j)),
                  pl.BlockSpec((tm, tn), lambda i, j: (i, j))],
        out_specs=pl.BlockSpec((tm, tn), lambda i, j: (i, j)),
        out_shape=jax.ShapeDtypeStruct((T, N), F32),
        compiler_params=_cparams(("parallel", "arbitrary")),
        name="proj_residual",
    )(a, w, res)


def _ffn_kernel(h_ref, g_ref, wg_ref, wu_ref, wd_ref, o_ref, hn_ref):
    f = pl.program_id(1)

    @pl.when(f == 0)
    def _():
        x = h_ref[...]
        r = lax.rsqrt(jnp.mean(x * x, axis=-1, keepdims=True) + NORM_EPS)
        hn_ref[...] = (x * r * g_ref[...]).astype(BF16)
        o_ref[...] = x

    hn = hn_ref[...]
    gate = jnp.dot(hn, wg_ref[...], preferred_element_type=F32)
    up = jnp.dot(hn, wu_ref[...], preferred_element_type=F32)
    act = (gate * jax.nn.sigmoid(gate) * up).astype(BF16)
    o_ref[...] += jnp.dot(act, wd_ref[...], preferred_element_type=F32)


def _dense_ffn(h2, norm_gain, wg, wu, wd, *, tm=1024, tf=512):
    T, D = h2.shape
    F = wg.shape[1]
    return pl.pallas_call(
        _ffn_kernel,
        grid=(T // tm, F // tf),
        in_specs=[pl.BlockSpec((tm, D), lambda i, f: (i, 0)),
                  pl.BlockSpec((1, D), lambda i, f: (0, 0)),
                  pl.BlockSpec((D, tf), lambda i, f: (0, f)),
                  pl.BlockSpec((D, tf), lambda i, f: (0, f)),
                  pl.BlockSpec((tf, D), lambda i, f: (f, 0))],
        out_specs=pl.BlockSpec((tm, D), lambda i, f: (i, 0)),
        out_shape=jax.ShapeDtypeStruct((T, D), F32),
        scratch_shapes=[pltpu.VMEM((tm, D), BF16)],
        compiler_params=_cparams(("parallel", "arbitrary")),
        name="dense_ffn",
    )(h2, norm_gain.reshape(1, D), wg, wu, wd)


def _router_kernel(h_ref, g_ref, wr_ref, br_ref, gate_ref):
    x = h_ref[...]
    r = lax.rsqrt(jnp.mean(x * x, axis=-1, keepdims=True) + NORM_EPS)
    hn = x * r * g_ref[...]
    logits = jnp.dot(hn, wr_ref[...], preferred_element_type=F32,
                     precision=lax.Precision.HIGHEST) + br_ref[...]
    lane = lax.broadcasted_iota(jnp.int32, logits.shape, 1)
    logits = jnp.where(lane < N_EXPERTS, logits, NEG_INF)
    m1 = jnp.max(logits, axis=-1, keepdims=True)
    i1 = jnp.min(jnp.where(logits == m1, lane, LANES), axis=-1, keepdims=True)
    rest = jnp.where(lane == i1, NEG_INF, logits)
    m2 = jnp.max(rest, axis=-1, keepdims=True)
    i2 = jnp.min(jnp.where(rest == m2, lane, LANES), axis=-1, keepdims=True)
    e2 = jnp.exp(m2 - m1)
    w1 = 1.0 / (1.0 + e2)
    w2 = e2 / (1.0 + e2)
    gate_ref[...] = (jnp.where(lane == 0, i1.astype(F32), 0.0) + jnp.where(lane == 1, i2.astype(F32), 0.0)
                     + jnp.where(lane == 2, w1, 0.0) + jnp.where(lane == 3, w2, 0.0))


def _router(h2, norm_gain, w_router, b_router, *, tm=512):
    T, D = h2.shape
    wr = jnp.zeros((D, LANES), F32).at[:, :N_EXPERTS].set(w_router)
    br = jnp.zeros((1, LANES), F32).at[0, :N_EXPERTS].set(b_router)
    return pl.pallas_call(
        _router_kernel,
        grid=(T // tm,),
        in_specs=[pl.BlockSpec((tm, D), lambda i: (i, 0)),
                  pl.BlockSpec((1, D), lambda i: (0, 0)),
                  pl.BlockSpec((D, LANES), lambda i: (0, 0)),
                  pl.BlockSpec((1, LANES), lambda i: (0, 0))],
        out_specs=pl.BlockSpec((tm, LANES), lambda i: (i, 0)),
        out_shape=jax.ShapeDtypeStruct((T, LANES), F32),
        compiler_params=_cparams(("parallel",)),
        name="moe_router",
    )(h2, norm_gain.reshape(1, D), wr, br)


MOE_TILE = 512


def _gather_rows_kernel(idx_hbm, src_hbm, o_ref, idx_smem, idx_sem, row_sem):
    i = pl.program_id(0)
    n = o_ref.shape[0]
    idx_copy = pltpu.make_async_copy(idx_hbm.at[i], idx_smem, idx_sem)
    idx_copy.start()
    idx_copy.wait()

    def row_copy(r, src_row):
        return pltpu.make_async_copy(src_hbm.at[pl.ds(src_row, 1), :], o_ref.at[pl.ds(r, 1), :], row_sem)

    def issue(r, carry):
        row_copy(r, idx_smem[r]).start()
        return carry

    def drain(r, carry):
        row_copy(r, 0).wait()
        return carry

    lax.fori_loop(0, n, issue, 0, unroll=8)
    lax.fori_loop(0, n, drain, 0, unroll=8)


def _gather_rows(src, idx):
    M = idx.shape[0]
    D = src.shape[1]
    n = MOE_TILE
    assert M % n == 0
    return pl.pallas_call(
        _gather_rows_kernel,
        grid=(M // n,),
        in_specs=[pl.BlockSpec(memory_space=pl.ANY), pl.BlockSpec(memory_space=pl.ANY)],
        out_specs=pl.BlockSpec((n, D), lambda i: (i, 0)),
        out_shape=jax.ShapeDtypeStruct((M, D), src.dtype),
        scratch_shapes=[pltpu.SMEM((n,), jnp.int32), pltpu.SemaphoreType.DMA, pltpu.SemaphoreType.DMA],
        compiler_params=_cparams(("arbitrary",)),
        name="gather_rows",
    )(idx.reshape(M // n, n), src)


def _moe_expert_kernel(te_ref, nu_ref, x_ref, g_ref, wg_ref, wu_ref, wd_ref, o_ref, hn_ref):
    i = pl.program_id(0)
    f = pl.program_id(1)

    @pl.when(f == 0)
    def _():
        x = x_ref[...]
        r = lax.rsqrt(jnp.mean(x * x, axis=-1, keepdims=True) + NORM_EPS)
        hn_ref[...] = (x * r * g_ref[...]).astype(BF16)
        o_ref[...] = jnp.zeros(o_ref.shape, F32)

    @pl.when(i < nu_ref[0])
    def _():
        hn = hn_ref[...]
        gate = jnp.dot(hn, wg_ref[0], preferred_element_type=F32)
        up = jnp.dot(hn, wu_ref[0], preferred_element_type=F32)
        act = (gate * jax.nn.sigmoid(gate) * up).astype(BF16)
        o_ref[...] += jnp.dot(act, wd_ref[0], preferred_element_type=F32)


def _moe_experts(xs, norm_gain, tile_expert, n_used, we_gate, we_up, we_down, *, tf=512):
    N, D = xs.shape
    F = we_gate.shape[2]
    tm = MOE_TILE
    grid_spec = pltpu.PrefetchScalarGridSpec(
        num_scalar_prefetch=2,
        grid=(N // tm, F // tf),
        in_specs=[pl.BlockSpec((tm, D), lambda i, f, te, nu: (i, 0)),
                  pl.BlockSpec((1, D), lambda i, f, te, nu: (0, 0)),
                  pl.BlockSpec((1, D, tf), lambda i, f, te, nu: (te[i], 0, f)),
                  pl.BlockSpec((1, D, tf), lambda i, f, te, nu: (te[i], 0, f)),
                  pl.BlockSpec((1, tf, D), lambda i, f, te, nu: (te[i], f, 0))],
        out_specs=pl.BlockSpec((tm, D), lambda i, f, te, nu: (i, 0)),
        scratch_shapes=[pltpu.VMEM((tm, D), BF16)])
    return pl.pallas_call(
        _moe_expert_kernel,
        grid_spec=grid_spec,
        out_shape=jax.ShapeDtypeStruct((N, D), F32),
        compiler_params=_cparams(("arbitrary", "arbitrary")),
        name="moe_experts",
    )(tile_expert, n_used, xs, norm_gain.reshape(1, D), we_gate, we_up, we_down)


def _moe_combine_kernel(h_ref, r_ref, y0_ref, y1_ref, o_ref):
    r = r_ref[...]
    o_ref[...] = h_ref[...] + r[:, 2:3] * y0_ref[...] + r[:, 3:4] * y1_ref[...]


def _moe_combine(h2, route, yg, *, tm=512):
    T, D = h2.shape
    nt = T // tm
    return pl.pallas_call(
        _moe_combine_kernel,
        grid=(nt,),
        in_specs=[pl.BlockSpec((tm, D), lambda i: (i, 0)),
                  pl.BlockSpec((tm, LANES), lambda i: (i, 0)),
                  pl.BlockSpec((tm, D), lambda i: (i, 0)),
                  pl.BlockSpec((tm, D), lambda i: (i + nt, 0))],
        out_specs=pl.BlockSpec((tm, D), lambda i: (i, 0)),
        out_shape=jax.ShapeDtypeStruct((T, D), F32),
        compiler_params=_cparams(("parallel",)),
        name="moe_combine",
    )(h2, route, yg, yg)


def _moe_ffn(h2, norm_gain, w_router, b_router, we_gate, we_up, we_down):
    T, D = h2.shape
    E = we_gate.shape[0]
    tm = MOE_TILE
    route = _router(h2, norm_gain, w_router, b_router)
    expert = route[:, 0:2].astype(jnp.int32).T.reshape(-1)
    onehot = (expert[:, None] == jnp.arange(E)[None, :]).astype(jnp.int32)
    rank = jnp.sum((jnp.cumsum(onehot, axis=0) - onehot) * onehot, axis=1)
    tiles = (jnp.sum(onehot, axis=0) + tm - 1) // tm
    tile_end = jnp.cumsum(tiles)
    pos = ((tile_end - tiles) * tm)[expert] + rank
    n_rows = 2 * T + E * tm
    row_token = jnp.zeros((n_rows,), jnp.int32).at[pos].set(jnp.tile(jnp.arange(T, dtype=jnp.int32), 2))
    tile_expert = jnp.minimum(jnp.sum(jnp.arange(n_rows // tm)[:, None] >= tile_end[None, :], axis=1),
                              E - 1).astype(jnp.int32)
    xs = _gather_rows(h2, row_token)
    ys = _moe_experts(xs, norm_gain, tile_expert, tile_end[E - 1:].astype(jnp.int32), we_gate, we_up, we_down)
    return _moe_combine(h2, route, _gather_rows(ys, pos.astype(jnp.int32)))


def _rel_bucket(dist):
    n = jnp.maximum(dist, 0)
    max_exact = REL_BUCKETS // 2
    nf = jnp.maximum(n, max_exact).astype(F32)
    large = max_exact + (jnp.log(nf / max_exact) / math.log(REL_MAX_DIST / max_exact)
                         * (REL_BUCKETS - max_exact)).astype(jnp.int32)
    return jnp.where(n < max_exact, n, jnp.minimum(large, REL_BUCKETS - 1))


def _bias_tiles_kernel(tab_ref, idx_ref, o_ref):
    t = idx_ref.shape[-1]

    def chunk(c, carry):
        rows = pl.ds(pl.multiple_of(c * 8, 8), 8)
        idx = idx_ref[0, rows, :]
        for h in range(N_SCORE_HEADS):
            tile = jnp.zeros((8, t), F32)
            for b in range(REL_BUCKETS):
                tile = jnp.where(idx == b, tab_ref[h, b] * LOG2E, tile)
            o_ref[h, 0, rows, :] = tile
        return carry

    lax.fori_loop(0, t // 8, chunk, 0)


def _bias_tiles(rel_bias, S, t):
    n_delta = S // t
    i = jnp.arange(t)
    dist = (jnp.arange(n_delta) * t)[:, None, None] + i[None, None, :] - i[None, :, None]
    idx = _rel_bucket(dist).astype(jnp.int32)
    return pl.pallas_call(
        _bias_tiles_kernel,
        grid=(n_delta,),
        in_specs=[pl.BlockSpec(memory_space=pltpu.SMEM),
                  pl.BlockSpec((1, t, t), lambda d: (d, 0, 0))],
        out_specs=pl.BlockSpec((N_SCORE_HEADS, 1, t, t), lambda d: (0, d, 0, 0)),
        out_shape=jax.ShapeDtypeStruct((N_SCORE_HEADS, n_delta, t, t), F32),
        compiler_params=_cparams(("parallel",)),
        name="bias_tiles",
    )(rel_bias.T, idx)


def _flash_steps(mls, acc_refs, scores, masks, v_ts):
    new_mls, alphas, probs = [], [], []
    for (m, l), s, mask in zip(mls, scores, masks):
        if mask is not None:
            s = jnp.where(mask, s, NEG_INF)
        m_new = jnp.maximum(m, jnp.max(s, axis=0, keepdims=True))
        alpha = jnp.exp2(m - m_new)
        p = jnp.exp2(s - m_new)
        if mask is not None:
            p = jnp.where(mask, p, 0.0)
        new_mls.append((m_new, alpha * l + jnp.sum(p, axis=0, keepdims=True)))
        alphas.append(alpha)
        probs.append(p.astype(BF16))
    for acc_ref, alpha, p, v_t in zip(acc_refs, alphas, probs, v_ts):
        acc_ref[...] = alpha * acc_ref[...] + jnp.dot(v_t, p, preferred_element_type=F32)
    return tuple(new_mls)


def _flash_init(nq):
    return (jnp.full((1, nq), NEG_INF, F32), jnp.zeros((1, nq), F32))


def _flash_out(ml, acc_ref):
    return acc_ref[...] / jnp.maximum(ml[1], 1e-30)


def _key_tile(kb, t):
    return pl.ds(pl.multiple_of(kb * t, t), t)


def _causal_t(t):
    return lax.broadcasted_iota(jnp.int32, (t, t), 1) >= lax.broadcasted_iota(jnp.int32, (t, t), 0)


def _row_band(x, lo, hi):
    row = lax.broadcasted_iota(jnp.int32, (x.shape[0], 1), 0)
    return jnp.where((row >= lo) & (row < hi), x, jnp.zeros_like(x))


def _rank_select(score, cand_valid, topk):
    C = score.shape[0]
    row = lax.broadcasted_iota(jnp.int32, score.shape, 0)
    rank = jnp.zeros(score.shape, F32)
    for c in range(C):
        sc = score[c:c + 1, :]
        tie = (row > c).astype(F32)
        rank = rank + jnp.where(sc > score, 1.0, jnp.where(sc == score, tie, 0.0))
    return jnp.where(cand_valid & (rank < topk), 1.0, 0.0)


def _diff_attn_kernel(lam_ref, sub_ref, qt_ref, k_ref, vt_ref, bias_ref, o_ref, acc_ref, *, lambda_init):
    t = ATT_TILE
    qb = pl.program_id(2)
    qt = qt_ref[0]
    qm = [_row_band(qt, mi * HEAD_DIM, (mi + 1) * HEAD_DIM) for mi in range(2)]
    acc_ref[...] = jnp.zeros(acc_ref.shape, F32)

    def step(kb, mls, mask):
        k = k_ref[0, _key_tile(kb, t), :]
        v_t = vt_ref[0, :, _key_tile(kb, t)]
        scores = [jnp.dot(k, qm[mi], preferred_element_type=F32) + bias_ref[mi, qb - kb] for mi in range(2)]
        return _flash_steps(mls, [acc_ref.at[mi] for mi in range(2)], scores, [mask] * 2, [v_t] * 2)

    mls = lax.fori_loop(0, qb, lambda kb, mls: step(kb, mls, None), (_flash_init(t), _flash_init(t)))
    mls = step(qb, mls, _causal_t(t))
    lv = lam_ref[...]
    lam = (jnp.exp(jnp.sum(lv[0:1] * lv[1:2], axis=-1, keepdims=True))
           - jnp.exp(jnp.sum(lv[2:3] * lv[3:4], axis=-1, keepdims=True)) + lambda_init)
    o = _flash_out(mls[0], acc_ref.at[0]) - lam * _flash_out(mls[1], acc_ref.at[1])
    r = lax.rsqrt(jnp.mean(o * o, axis=0, keepdims=True) + NORM_EPS)
    o_ref[0] = (o * r * sub_ref[...] * (1.0 - lambda_init)).astype(o_ref.dtype)


def _diff_attention(q_t, qkv, v_t, lam4, subln_gain, bias_tiles, lambda_init):
    B, S, _ = qkv.shape
    t = ATT_TILE
    nq = S // t
    H = DIFF_HEADS
    return pl.pallas_call(
        functools.partial(_diff_attn_kernel, lambda_init=lambda_init),
        grid=(H, B, nq),
        in_specs=[pl.BlockSpec((4, HEAD_DIM), lambda h, b, i: (0, 0)),
                  pl.BlockSpec((LANES, 1), lambda h, b, i: (0, 0)),
                  pl.BlockSpec((1, LANES, t), lambda h, b, i: (b, h, i)),
                  pl.BlockSpec((1, S, LANES), lambda h, b, i: (b, 0, H + h)),
                  pl.BlockSpec((1, LANES, S), lambda h, b, i: (b, h, 0)),
                  pl.BlockSpec((2, nq, t, t), lambda h, b, i: (h, 0, 0, 0))],
        out_specs=pl.BlockSpec((1, LANES, t), lambda h, b, i: (b, h, i)),
        out_shape=jax.ShapeDtypeStruct((B, H * LANES, S), BF16),
        scratch_shapes=[pltpu.VMEM((2, LANES, t), F32)],
        compiler_params=_cparams(("parallel", "parallel", "arbitrary")),
        name="diff_attention",
    )(lam4, subln_gain.reshape(LANES, 1), q_t, qkv, v_t, bias_tiles)


def _moba_kernel(qt_ref, k_ref, vt_ref, bias_ref, o_ref, kmean_ref, sel_ref, acc_ref):
    t = ATT_TILE
    qb = pl.program_id(2)
    n_blk = kmean_ref.shape[0]

    @pl.when(qb == 0)
    def _():
        for n in range(n_blk):
            kb = k_ref[0, n * t:(n + 1) * t, :].astype(F32)
            kmean_ref[n:n + 1, :] = jnp.mean(kb, axis=0, keepdims=True)

    qt = qt_ref[0]
    qm = [_row_band(qt, hj * HEAD_DIM, (hj + 1) * HEAD_DIM) for hj in range(2)]
    km = kmean_ref[...]
    km_hi = km.astype(BF16)
    km_lo = (km - km_hi.astype(F32)).astype(BF16)
    past = lax.broadcasted_iota(jnp.int32, (n_blk, t), 0) < qb
    for hj in range(2):
        gate = (jnp.dot(km_hi, qm[hj], preferred_element_type=F32)
                + jnp.dot(km_lo, qm[hj], preferred_element_type=F32))
        sel_ref[hj] = _rank_select(jnp.where(past, gate, NEG_INF), past, MOBA_TOPK)
    acc_ref[...] = jnp.zeros(acc_ref.shape, F32)

    def step(kb, mls, diagonal):
        k = k_ref[0, _key_tile(kb, t), :]
        v_t = vt_ref[0, :, _key_tile(kb, t)]
        scores = [jnp.dot(k, qm[hj], preferred_element_type=F32) + bias_ref[hj, qb - kb] for hj in range(2)]
        masks = [_causal_t(t) if diagonal else sel_ref[hj, pl.ds(kb, 1), :] > 0.5 for hj in range(2)]
        return _flash_steps(mls, [acc_ref.at[hj] for hj in range(2)], scores, masks,
                            [v_t[hj * HEAD_DIM:(hj + 1) * HEAD_DIM, :] for hj in range(2)])

    mls = lax.fori_loop(0, qb, lambda kb, mls: step(kb, mls, False), (_flash_init(t), _flash_init(t)))
    mls = step(qb, mls, True)
    for hj in range(2):
        o_ref[0, hj * HEAD_DIM:(hj + 1) * HEAD_DIM, :] = _flash_out(mls[hj], acc_ref.at[hj]).astype(o_ref.dtype)


def _moba_attention(q_t, qkv, v_t, bias_tiles):
    B, S, _ = qkv.shape
    t = ATT_TILE
    assert t == MOBA_BLOCK and S % t == 0
    nq = S // t
    HP = D_MODEL // LANES
    return pl.pallas_call(
        _moba_kernel,
        grid=(HP, B, nq),
        in_specs=[pl.BlockSpec((1, LANES, t), lambda h, b, i: (b, h, i)),
                  pl.BlockSpec((1, S, LANES), lambda h, b, i: (b, 0, HP + h)),
                  pl.BlockSpec((1, LANES, S), lambda h, b, i: (b, h, 0)),
                  pl.BlockSpec((2, nq, t, t), lambda h, b, i: (h, 0, 0, 0))],
        out_specs=pl.BlockSpec((1, LANES, t), lambda h, b, i: (b, h, i)),
        out_shape=jax.ShapeDtypeStruct((B, D_MODEL, S), BF16),
        scratch_shapes=[pltpu.VMEM((S // t, LANES), F32),
                        pltpu.VMEM((2, S // t, t), F32),
                        pltpu.VMEM((2, HEAD_DIM, t), F32)],
        compiler_params=_cparams(("parallel", "parallel", "arbitrary")),
        name="moba_attention",
    )(q_t, qkv, v_t, bias_tiles)


def _compress_kernel(x_ref, pos_ref, w1_ref, w2_ref, gain_ref, o_ref, *, normalize):
    G, n_half, width = x_ref.shape[1:]
    x = x_ref[0].reshape(G * n_half, width).astype(F32)
    first = (x + pos_ref[0:1, :]).astype(BF16)
    second = (x + pos_ref[1:2, :]).astype(BF16)
    u = jnp.dot(first, w1_ref[0:width, :], preferred_element_type=F32)
    low = jnp.dot(second, w1_ref[width:2 * width, :], preferred_element_type=F32)
    rows = G * n_half
    hid = u + pltpu.roll(low, rows - 1, 0)
    c = jnp.dot(jax.nn.gelu(hid).astype(BF16), w2_ref[...], preferred_element_type=F32)
    if normalize:
        r = lax.rsqrt(jnp.mean(c * c, axis=-1, keepdims=True) + NORM_EPS)
        c = c * r * gain_ref[...]
    o_ref[0] = c.reshape(G, n_half, HEAD_DIM).astype(o_ref.dtype)


def _compress(x, pos, w1, w2, gain, normalize):
    B, G, S, d = x.shape
    n_half = S // NSA_CMP_STRIDE
    width = NSA_CMP_STRIDE * d
    return pl.pallas_call(
        functools.partial(_compress_kernel, normalize=normalize),
        grid=(B,),
        in_specs=[pl.BlockSpec((1, G, n_half, width), lambda b: (b, 0, 0, 0)),
                  pl.BlockSpec((2, width), lambda b: (0, 0)),
                  pl.BlockSpec((2 * width, NSA_CMP_HIDDEN), lambda b: (0, 0)),
                  pl.BlockSpec((NSA_CMP_HIDDEN, d), lambda b: (0, 0)),
                  pl.BlockSpec((1, d), lambda b: (0, 0))],
        out_specs=pl.BlockSpec((1, G, n_half, d), lambda b: (b, 0, 0, 0)),
        out_shape=jax.ShapeDtypeStruct((B, G, n_half, d), BF16),
        compiler_params=_cparams(("parallel",)),
        name="nsa_compress",
    )(x.reshape(B, G, n_half, width), pos.reshape(2, width), w1, w2, gain.reshape(1, d))


def _nsa_kernel(qt_ref, kc_ref, vct_ref, ks_ref, vst_ref, kw_ref, vwt_ref, g_ref, bias_ref,
                ov_ref, o_ref, sel_ref, acc_ref, ocmp_ref, osel_ref):
    t = ATT_TILE
    R = NSA_GROUP_SIZE
    d = HEAD_DIM
    qb = pl.program_id(2)
    n_cmp_rows = kc_ref.shape[2]
    n_sel = ov_ref.shape[0]
    qt = qt_ref[0]
    qh = [qt[r * d:(r + 1) * d, :] for r in range(R)]
    qpos = qb * t + lax.broadcasted_iota(jnp.int32, (1, t), 1)

    n_idx = lax.broadcasted_iota(jnp.int32, (n_cmp_rows, 1), 0)
    cmask = (n_idx * NSA_CMP_STRIDE + (NSA_CMP_BLOCK - 1) <= qpos) & (n_idx < n_cmp_rows - 1)
    kc = kc_ref[0, 0]
    psum = jnp.zeros((n_cmp_rows, t), F32)
    for r in range(R):
        cl = jnp.where(cmask, jnp.dot(kc, qh[r], preferred_element_type=F32), NEG_INF)
        cm = jnp.max(cl, axis=0, keepdims=True)
        cp = jnp.where(cmask, jnp.exp2(cl - cm), 0.0)
        cp = cp / jnp.maximum(jnp.sum(cp, axis=0, keepdims=True), 1e-30)
        ocmp_ref[r] = jnp.dot(vct_ref[0, 0], cp.astype(BF16), preferred_element_type=F32)
        psum = psum + cp

    p_hi = psum.astype(BF16)
    p_lo = (psum - p_hi.astype(F32)).astype(BF16)
    imp = (jnp.dot(ov_ref[...], p_hi, preferred_element_type=F32)
           + jnp.dot(ov_ref[...], p_lo, preferred_element_type=F32))
    blk = lax.broadcasted_iota(jnp.int32, (n_sel, 1), 0)
    cur = qpos // NSA_SEL_BLOCK
    forced = (blk == 0) | (blk == cur) | (blk == cur - 1)
    score = jnp.where(forced, FORCE_SCORE, jnp.where(blk <= cur, imp, NEG_INF))
    sel_ref[...] = _rank_select(score, blk <= cur, NSA_SEL_TOPK)

    causal = _causal_t(t)
    per_tile = t // NSA_SEL_BLOCK

    def sweep(k_ref, vt_ref, kb, delta, mls, mask):
        k = k_ref[0, 0, _key_tile(kb, t), :]
        v_t = vt_ref[0, :, _key_tile(kb, t)]
        scores = [jnp.dot(k, qh[r], preferred_element_type=F32) + bias_ref[r, delta] for r in range(R)]
        return _flash_steps(mls, [acc_ref.at[r] for r in range(R)], scores, [mask] * R, [v_t] * R)

    def chosen_rows(kb):
        rows = [jnp.broadcast_to(sel_ref[pl.ds(kb * per_tile + c, 1), :], (NSA_SEL_BLOCK, t))
                for c in range(per_tile)]
        return jnp.concatenate(rows, axis=0) > 0.5

    init = tuple(_flash_init(t) for _ in range(R))

    acc_ref[...] = jnp.zeros(acc_ref.shape, F32)
    mls = lax.fori_loop(0, qb, lambda kb, mls: sweep(ks_ref, vst_ref, kb, qb - kb, mls, chosen_rows(kb)), init)
    mls = sweep(ks_ref, vst_ref, qb, 0, mls, chosen_rows(qb) & causal)
    for r in range(R):
        osel_ref[r] = _flash_out(mls[r], acc_ref.at[r])

    acc_ref[...] = jnp.zeros(acc_ref.shape, F32)
    key_j = lax.broadcasted_iota(jnp.int32, (t, t), 0)
    qry_i = lax.broadcasted_iota(jnp.int32, (t, t), 1)
    mls = init
    for back in range(NSA_WINDOW // t, -1, -1):
        dist = back * t + qry_i - key_j
        wmask = (dist >= 0) & (dist < NSA_WINDOW) & (qb >= back)
        mls = sweep(kw_ref, vwt_ref, jnp.maximum(qb - back, 0), back, mls, wmask)

    gates = jax.nn.sigmoid(g_ref[0, 0].astype(F32))
    for r in range(R):
        o = (gates[3 * r:3 * r + 1, :] * ocmp_ref[r]
             + gates[3 * r + 1:3 * r + 2, :] * osel_ref[r]
             + gates[3 * r + 2:3 * r + 3, :] * _flash_out(mls[r], acc_ref.at[r]))
        o_ref[0, r * d:(r + 1) * d, :] = o.astype(o_ref.dtype)


def _nsa_overlap_t(S):
    n_cmp_rows = S // NSA_CMP_STRIDE
    n_sel = S // NSA_SEL_BLOCK
    cmp_start = jnp.arange(n_cmp_rows) * NSA_CMP_STRIDE
    sel_start = jnp.arange(n_sel) * NSA_SEL_BLOCK
    overlap = jnp.clip(jnp.minimum(cmp_start[None, :] + NSA_CMP_BLOCK, sel_start[:, None] + NSA_SEL_BLOCK)
                       - jnp.maximum(cmp_start[None, :], sel_start[:, None]), 0).astype(F32) / NSA_CMP_STRIDE
    return jnp.where(cmp_start[None, :] + NSA_CMP_BLOCK <= S, overlap, 0.0).astype(BF16)


def _nsa_attention(q_t, kc, vc_t, ks, vs_t, kw, vw_t, gates_t, bias_tiles):
    B, _, S = q_t.shape
    t = ATT_TILE
    nq = S // t
    G, R, d = NSA_KV_GROUPS, NSA_GROUP_SIZE, HEAD_DIM
    n_cmp_rows = kc.shape[2]
    overlap_t = _nsa_overlap_t(S)
    n_sel = overlap_t.shape[0]
    k_spec = pl.BlockSpec((1, 1, S, d), lambda g, b, i: (b, g, 0, 0))
    vt_spec = pl.BlockSpec((1, d, S), lambda g, b, i: (b, g, 0))
    return pl.pallas_call(
        _nsa_kernel,
        grid=(G, B, nq),
        in_specs=[pl.BlockSpec((1, R * d, t), lambda g, b, i: (b, g, i)),
                  pl.BlockSpec((1, 1, n_cmp_rows, d), lambda g, b, i: (b, g, 0, 0)),
                  pl.BlockSpec((1, 1, d, n_cmp_rows), lambda g, b, i: (b, g, 0, 0)),
                  k_spec, vt_spec, k_spec, vt_spec,
                  pl.BlockSpec((1, 1, 3 * R, t), lambda g, b, i: (b, g, 0, i)),
                  pl.BlockSpec((R, nq, t, t), lambda g, b, i: (g, 0, 0, 0)),
                  pl.BlockSpec((n_sel, n_cmp_rows), lambda g, b, i: (0, 0))],
        out_specs=pl.BlockSpec((1, R * d, t), lambda g, b, i: (b, g, i)),
        out_shape=jax.ShapeDtypeStruct((B, G * R * d, S), BF16),
        scratch_shapes=[pltpu.VMEM((n_sel, t), F32),
                        pltpu.VMEM((R, d, t), F32),
                        pltpu.VMEM((R, d, t), F32),
                        pltpu.VMEM((R, d, t), F32)],
        compiler_params=_cparams(("parallel", "parallel", "arbitrary")),
        name="nsa_attention",
    )(q_t, kc, vc_t, ks, vs_t, kw, vw_t, gates_t, bias_tiles, overlap_t)


def _qkv_head_gain(q_gain, k_gain, n_heads, n_plain):
    return jnp.concatenate([jnp.tile(q_gain, n_heads) * Q_SCALE, jnp.tile(k_gain, n_heads),
                            jnp.ones((n_plain,), F32)])


def _swap_last(x):
    return x.transpose(0, 2, 1)


def _diff_layer(h2, B, S, norm_mix, p, bias_tiles, lambda_init):
    w_in, w_out, q_gain, k_gain, lq1, lk1, lq2, lk2, subln = p
    hg = _qkv_head_gain(q_gain, k_gain, 2 * DIFF_HEADS, D_MODEL)
    qkv = _norm_proj(h2, norm_mix, w_in.astype(BF16), hg, 2 * D_MODEL).reshape(B, S, -1)
    o_t = _diff_attention(_swap_last(qkv[:, :, :D_MODEL]), qkv, _swap_last(qkv[:, :, 2 * D_MODEL:]),
                          jnp.stack([lq1, lk1, lq2, lk2]), subln, bias_tiles, lambda_init)
    return _proj_residual(_swap_last(o_t).reshape(B * S, -1), w_out.astype(BF16), h2)


def _moba_layer(h2, B, S, norm_mix, p, bias_tiles):
    w_in, w_out, q_gain, k_gain = p
    hg = _qkv_head_gain(q_gain, k_gain, N_SCORE_HEADS, D_MODEL)
    qkv = _norm_proj(h2, norm_mix, w_in.astype(BF16), hg, 2 * D_MODEL).reshape(B, S, -1)
    o_t = _moba_attention(_swap_last(qkv[:, :, :D_MODEL]), qkv, _swap_last(qkv[:, :, 2 * D_MODEL:]), bias_tiles)
    return _proj_residual(_swap_last(o_t).reshape(B * S, -1), w_out.astype(BF16), h2)


def _nsa_layer(h2, B, S, norm_mix, p, bias_tiles):
    w_in, w_out, q_gain, k_gain, pos_k, pos_v, k_w1, k_w2, v_w1, v_w2 = p
    G, d = NSA_KV_GROUPS, HEAD_DIM
    kvw = G * d
    qw = N_SCORE_HEADS * d
    sec = {name: slice(qw + i * kvw, qw + (i + 1) * kvw) for i, name in
           enumerate(("kc", "vc", "ks", "vs", "kw", "vw"))}
    n_gate = 3 * N_SCORE_HEADS
    tn = 384
    n_used = qw + 6 * kvw + n_gate
    n_pad = -n_used % tn
    w_perm = jnp.concatenate([w_in[:, :qw], w_in[:, sec["ks"]], w_in[:, sec["kw"]], w_in[:, sec["kc"]],
                              w_in[:, sec["vc"]], w_in[:, sec["vs"]], w_in[:, sec["vw"]],
                              w_in[:, qw + 6 * kvw:], jnp.zeros((D_MODEL, n_pad), F32)], axis=1)
    hg = jnp.concatenate([jnp.tile(q_gain, N_SCORE_HEADS) * Q_SCALE, jnp.tile(k_gain[1], G),
                          jnp.tile(k_gain[2], G), jnp.ones((n_used + n_pad - qw - 2 * kvw,), F32)])
    proj = _norm_proj(h2, norm_mix, w_perm.astype(BF16), hg, qw + 2 * kvw, tn=tn).reshape(B, S, -1)

    def section(i):
        return proj[:, :, qw + i * kvw:qw + (i + 1) * kvw]

    def group_major(x):
        return x.reshape(B, S, G, d).transpose(0, 2, 1, 3)

    ks, kw, kc, vc, vs, vw = (section(i) for i in range(6))
    gates_t = _swap_last(proj[:, :, qw + 6 * kvw:n_used]).reshape(B, G, n_gate // G, S)
    k_cmp = _compress(group_major(kc), pos_k, k_w1.astype(BF16), k_w2.astype(BF16), k_gain[0], True)
    v_cmp = _compress(group_major(vc), pos_v, v_w1.astype(BF16), v_w2.astype(BF16), k_gain[0], False)
    o_t = _nsa_attention(_swap_last(proj[:, :, :qw]), k_cmp, v_cmp.transpose(0, 1, 3, 2),
                         group_major(ks), _swap_last(vs), group_major(kw), _swap_last(vw), gates_t, bias_tiles)
    return _proj_residual(_swap_last(o_t).reshape(B * S, qw), w_out.astype(BF16), h2)


def _diff_lambda_init(layer):
    return 0.8 - 0.6 * math.exp(-0.3 * layer)


def kernel(x, rel_bias,
           l0_norm_mix, l0_w_in, l0_w_out, l0_q_gain, l0_k_gain, l0_lam_q1, l0_lam_k1, l0_lam_q2, l0_lam_k2,
           l0_subln_gain, l0_norm_ffn, l0_w_gate, l0_w_up, l0_w_down,
           l1_norm_mix, l1_w_in, l1_w_out, l1_q_gain, l1_k_gain, l1_norm_ffn, l1_w_router, l1_b_router,
           l1_we_gate, l1_we_up, l1_we_down,
           l2_norm_mix, l2_w_in, l2_w_out, l2_q_gain, l2_k_gain, l2_cmp_pos_k, l2_cmp_pos_v, l2_cmp_k_w1,
           l2_cmp_k_w2, l2_cmp_v_w1, l2_cmp_v_w2, l2_norm_ffn, l2_w_gate, l2_w_up, l2_w_down,
           l3_norm_mix, l3_w_in, l3_w_out, l3_q_gain, l3_k_gain, l3_lam_q1, l3_lam_k1, l3_lam_q2, l3_lam_k2,
           l3_subln_gain, l3_norm_ffn, l3_w_router, l3_b_router, l3_we_gate, l3_we_up, l3_we_down):
    B, S, D = x.shape
    bias_tiles = _bias_tiles(rel_bias, S, ATT_TILE)
    h = x.reshape(B * S, D)

    h = _diff_layer(h, B, S, l0_norm_mix, (l0_w_in, l0_w_out, l0_q_gain, l0_k_gain, l0_lam_q1, l0_lam_k1,
                                           l0_lam_q2, l0_lam_k2, l0_subln_gain), bias_tiles, _diff_lambda_init(0))
    h = _dense_ffn(h, l0_norm_ffn, l0_w_gate.astype(BF16), l0_w_up.astype(BF16), l0_w_down.astype(BF16))

    h = _moba_layer(h, B, S, l1_norm_mix, (l1_w_in, l1_w_out, l1_q_gain, l1_k_gain), bias_tiles)
    h = _moe_ffn(h, l1_norm_ffn, l1_w_router, l1_b_router, l1_we_gate.astype(BF16), l1_we_up.astype(BF16),
                 l1_we_down.astype(BF16))

    h = _nsa_layer(h, B, S, l2_norm_mix, (l2_w_in, l2_w_out, l2_q_gain, l2_k_gain, l2_cmp_pos_k, l2_cmp_pos_v,
                                          l2_cmp_k_w1, l2_cmp_k_w2, l2_cmp_v_w1, l2_cmp_v_w2), bias_tiles)
    h = _dense_ffn(h, l2_norm_ffn, l2_w_gate.astype(BF16), l2_w_up.astype(BF16), l2_w_down.astype(BF16))

    h = _diff_layer(h, B, S, l3_norm_mix, (l3_w_in, l3_w_out, l3_q_gain, l3_k_gain, l3_lam_q1, l3_lam_k1,
                                           l3_lam_q2, l3_lam_k2, l3_subln_gain), bias_tiles, _diff_lambda_init(3))
    h = _moe_ffn(h, l3_norm_ffn, l3_w_router, l3_b_router, l3_we_gate.astype(BF16), l3_we_up.astype(BF16),
                 l3_we_down.astype(BF16))
    return h.reshape(B, S, D)
```

```python
import functools
import math

import jax
import jax.numpy as jnp
from jax import lax
from jax.experimental import pallas as pl
from jax.experimental.pallas import tpu as pltpu

F32 = jnp.float32
BF16 = jnp.bfloat16

D_MODEL = 1024
HEAD_DIM = 64
N_SCORE_HEADS = 16
DIFF_HEADS = 8
MOBA_BLOCK = 256
MOBA_TOPK = 3
NSA_KV_GROUPS = 4
NSA_GROUP_SIZE = 4
NSA_CMP_BLOCK = 32
NSA_CMP_STRIDE = 16
NSA_CMP_HIDDEN = 256
NSA_SEL_BLOCK = 64
NSA_SEL_TOPK = 16
NSA_WINDOW = 512
REL_BUCKETS = 32
REL_MAX_DIST = 1024
FFN_DIM = 3584
N_EXPERTS = 8
NORM_EPS = 1e-6
NEG_INF = -1e30
FORCE_SCORE = 1e30
ATTN_SCALE = HEAD_DIM ** -0.5
LOG2E = math.log2(math.e)
Q_SCALE = ATTN_SCALE * LOG2E

LANES = 128
ATT_TILE = 256
ATT_BATCH = 2
VMEM_LIMIT = 56 * 1024 * 1024


def _cparams(sem):
    return pltpu.CompilerParams(dimension_semantics=sem, vmem_limit_bytes=VMEM_LIMIT)


def _dot_nt(a, b):
    return lax.dot_general(a, b, (((1,), (1,)), ((), ())), preferred_element_type=F32)


def _norm_proj_kernel(x_ref, g_ref, w_ref, hg_ref, o_ref, xn_ref, *, n_norm_tiles, tn):
    j = pl.program_id(1)

    @pl.when(j == 0)
    def _():
        x = x_ref[...]
        r = lax.rsqrt(jnp.mean(x * x, axis=-1, keepdims=True) + NORM_EPS)
        xn_ref[...] = (x * r * g_ref[...]).astype(BF16)

    acc = jnp.dot(xn_ref[...], w_ref[...], preferred_element_type=F32)

    @pl.when(j < n_norm_tiles)
    def _():
        for s in range(tn // HEAD_DIM):
            sl = slice(s * HEAD_DIM, (s + 1) * HEAD_DIM)
            seg = acc[:, sl]
            r = lax.rsqrt(jnp.mean(seg * seg, axis=-1, keepdims=True) + NORM_EPS)
            o_ref[:, sl] = (seg * r * hg_ref[:, sl]).astype(o_ref.dtype)

    @pl.when(j >= n_norm_tiles)
    def _():
        o_ref[...] = acc.astype(o_ref.dtype)


def _norm_proj(h2, norm_gain, w, head_gain, n_norm_cols, *, tm=512, tn=512):
    T, D = h2.shape
    N = w.shape[1]
    assert T % tm == 0 and N % tn == 0 and n_norm_cols % tn == 0
    return pl.pallas_call(
        functools.partial(_norm_proj_kernel, n_norm_tiles=n_norm_cols // tn, tn=tn),
        grid=(T // tm, N // tn),
        in_specs=[pl.BlockSpec((tm, D), lambda i, j: (i, 0)),
                  pl.BlockSpec((1, D), lambda i, j: (0, 0)),
                  pl.BlockSpec((D, tn), lambda i, j: (0, j)),
                  pl.BlockSpec((1, tn), lambda i, j: (0, j))],
        out_specs=pl.BlockSpec((tm, tn), lambda i, j: (i, j)),
        out_shape=jax.ShapeDtypeStruct((T, N), BF16),
        scratch_shapes=[pltpu.VMEM((tm, D), BF16)],
        compiler_params=_cparams(("parallel", "arbitrary")),
        name="norm_proj",
    )(h2, norm_gain.reshape(1, D), w, head_gain.reshape(1, N))


def _proj_res_kernel(a_ref, w_ref, r_ref, o_ref):
    o_ref[...] = r_ref[...] + jnp.dot(a_ref[...], w_ref[...], preferred_element_type=F32)


def _proj_residual(a, w, res, *, tm=512, tn=512):
    T, K = a.shape
    N = w.shape[1]
    return pl.pallas_call(
        _proj_res_kernel,
        grid=(T // tm, N // tn),
        in_specs=[pl.BlockSpec((tm, K), lambda i, j: (i, 0)),
                  pl.BlockSpec((K, tn), lambda i, j: (0, j)),
                  pl.BlockSpec((tm, tn), lambda i, j: (i, j))],
        out_specs=pl.BlockSpec((tm, tn), lambda i, j: (i, j)),
        out_shape=jax.ShapeDtypeStruct((T, N), F32),
        compiler_params=_cparams(("parallel", "arbitrary")),
        name="proj_residual",
    )(a, w, res)


def _ffn_kernel(h_ref, g_ref, wg_ref, wu_ref, wd_ref, o_ref, hn_ref):
    f = pl.program_id(1)

    @pl.when(f == 0)
    def _():
        x = h_ref[...]
        r = lax.rsqrt(jnp.mean(x * x, axis=-1, keepdims=True) + NORM_EPS)
        hn_ref[...] = (x * r * g_ref[...]).astype(BF16)
        o_ref[...] = x

    hn = hn_ref[...]
    gate = jnp.dot(hn, wg_ref[...], preferred_element_type=F32)
    up = jnp.dot(hn, wu_ref[...], preferred_element_type=F32)
    act = (gate * jax.nn.sigmoid(gate) * up).astype(BF16)
    o_ref[...] += jnp.dot(act, wd_ref[...], preferred_element_type=F32)


def _dense_ffn(h2, norm_gain, wg, wu, wd, *, tm=1024, tf=512):
    T, D = h2.shape
    F = wg.shape[1]
    return pl.pallas_call(
        _ffn_kernel,
        grid=(T // tm, F // tf),
        in_specs=[pl.BlockSpec((tm, D), lambda i, f: (i, 0)),
                  pl.BlockSpec((1, D), lambda i, f: (0, 0)),
                  pl.BlockSpec((D, tf), lambda i, f: (0, f)),
                  pl.BlockSpec((D, tf), lambda i, f: (0, f)),
                  pl.BlockSpec((tf, D), lambda i, f: (f, 0))],
        out_specs=pl.BlockSpec((tm, D), lambda i, f: (i, 0)),
        out_shape=jax.ShapeDtypeStruct((T, D), F32),
        scratch_shapes=[pltpu.VMEM((tm, D), BF16)],
        compiler_params=_cparams(("parallel", "arbitrary")),
        name="dense_ffn",
    )(h2, norm_gain.reshape(1, D), wg, wu, wd)


def _router_kernel(h_ref, g_ref, wr_ref, br_ref, gate_ref):
    x = h_ref[...]
    r = lax.rsqrt(jnp.mean(x * x, axis=-1, keepdims=True) + NORM_EPS)
    hn = x * r * g_ref[...]
    logits = jnp.dot(hn, wr_ref[...], preferred_element_type=F32,
                     precision=lax.Precision.HIGHEST) + br_ref[...]
    lane = lax.broadcasted_iota(jnp.int32, logits.shape, 1)
    logits = jnp.where(lane < N_EXPERTS, logits, NEG_INF)
    m1 = jnp.max(logits, axis=-1, keepdims=True)
    i1 = jnp.min(jnp.where(logits == m1, lane, LANES), axis=-1, keepdims=True)
    rest = jnp.where(lane == i1, NEG_INF, logits)
    m2 = jnp.max(rest, axis=-1, keepdims=True)
    i2 = jnp.min(jnp.where(rest == m2, lane, LANES), axis=-1, keepdims=True)
    e2 = jnp.exp(m2 - m1)
    w1 = 1.0 / (1.0 + e2)
    w2 = e2 / (1.0 + e2)
    gate_ref[...] = (jnp.where(lane == 0, i1.astype(F32), 0.0) + jnp.where(lane == 1, i2.astype(F32), 0.0)
                     + jnp.where(lane == 2, w1, 0.0) + jnp.where(lane == 3, w2, 0.0))


def _router(h2, norm_gain, w_router, b_router, *, tm=512):
    T, D = h2.shape
    wr = jnp.zeros((D, LANES), F32).at[:, :N_EXPERTS].set(w_router)
    br = jnp.zeros((1, LANES), F32).at[0, :N_EXPERTS].set(b_router)
    return pl.pallas_call(
        _router_kernel,
        grid=(T // tm,),
        in_specs=[pl.BlockSpec((tm, D), lambda i: (i, 0)),
                  pl.BlockSpec((1, D), lambda i: (0, 0)),
                  pl.BlockSpec((D, LANES), lambda i: (0, 0)),
                  pl.BlockSpec((1, LANES), lambda i: (0, 0))],
        out_specs=pl.BlockSpec((tm, LANES), lambda i: (i, 0)),
        out_shape=jax.ShapeDtypeStruct((T, LANES), F32),
        compiler_params=_cparams(("parallel",)),
        name="moe_router",
    )(h2, norm_gain.reshape(1, D), wr, br)


MOE_TILE = 512


def _gather_rows_kernel(idx_hbm, src_hbm, o_ref, idx_smem, idx_sem, row_sem):
    i = pl.program_id(0)
    n = o_ref.shape[0]
    idx_copy = pltpu.make_async_copy(idx_hbm.at[i], idx_smem, idx_sem)
    idx_copy.start()
    idx_copy.wait()

    def row_copy(r, src_row):
        return pltpu.make_async_copy(src_hbm.at[pl.ds(src_row, 1), :], o_ref.at[pl.ds(r, 1), :], row_sem)

    def issue(r, carry):
        row_copy(r, idx_smem[r]).start()
        return carry

    def drain(r, carry):
        row_copy(r, 0).wait()
        return carry

    lax.fori_loop(0, n, issue, 0, unroll=8)
    lax.fori_loop(0, n, drain, 0, unroll=8)


def _gather_rows(src, idx):
    M = idx.shape[0]
    D = src.shape[1]
    n = MOE_TILE
    assert M % n == 0
    return pl.pallas_call(
        _gather_rows_kernel,
        grid=(M // n,),
        in_specs=[pl.BlockSpec(memory_space=pl.ANY), pl.BlockSpec(memory_space=pl.ANY)],
        out_specs=pl.BlockSpec((n, D), lambda i: (i, 0)),
        out_shape=jax.ShapeDtypeStruct((M, D), src.dtype),
        scratch_shapes=[pltpu.SMEM((n,), jnp.int32), pltpu.SemaphoreType.DMA, pltpu.SemaphoreType.DMA],
        compiler_params=_cparams(("arbitrary",)),
        name="gather_rows",
    )(idx.reshape(M // n, n), src)


def _moe_expert_kernel(te_ref, nu_ref, x_ref, g_ref, wg_ref, wu_ref, wd_ref, o_ref, hn_ref):
    i = pl.program_id(0)
    f = pl.program_id(1)

    @pl.when(f == 0)
    def _():
        x = x_ref[...]
        r = lax.rsqrt(jnp.mean(x * x, axis=-1, keepdims=True) + NORM_EPS)
        hn_ref[...] = (x * r * g_ref[...]).astype(BF16)
        o_ref[...] = jnp.zeros(o_ref.shape, F32)

    @pl.when(i < nu_ref[0])
    def _():
        hn = hn_ref[...]
        gate = jnp.dot(hn, wg_ref[0], preferred_element_type=F32)
        up = jnp.dot(hn, wu_ref[0], preferred_element_type=F32)
        act = (gate * jax.nn.sigmoid(gate) * up).astype(BF16)
        o_ref[...] += jnp.dot(act, wd_ref[0], preferred_element_type=F32)


def _moe_experts(xs, norm_gain, tile_expert, n_used, we_gate, we_up, we_down, *, tf=512):
    N, D = xs.shape
    F = we_gate.shape[2]
    tm = MOE_TILE
    grid_spec = pltpu.PrefetchScalarGridSpec(
        num_scalar_prefetch=2,
        grid=(N // tm, F // tf),
        in_specs=[pl.BlockSpec((tm, D), lambda i, f, te, nu: (i, 0)),
                  pl.BlockSpec((1, D), lambda i, f, te, nu: (0, 0)),
                  pl.BlockSpec((1, D, tf), lambda i, f, te, nu: (te[i], 0, f)),
                  pl.BlockSpec((1, D, tf), lambda i, f, te, nu: (te[i], 0, f)),
                  pl.BlockSpec((1, tf, D), lambda i, f, te, nu: (te[i], f, 0))],
        out_specs=pl.BlockSpec((tm, D), lambda i, f, te, nu: (i, 0)),
        scratch_shapes=[pltpu.VMEM((tm, D), BF16)])
    return pl.pallas_call(
        _moe_expert_kernel,
        grid_spec=grid_spec,
        out_shape=jax.ShapeDtypeStruct((N, D), F32),
        compiler_params=_cparams(("arbitrary", "arbitrary")),
        name="moe_experts",
    )(tile_expert, n_used, xs, norm_gain.reshape(1, D), we_gate, we_up, we_down)


def _moe_combine_kernel(h_ref, r_ref, y0_ref, y1_ref, o_ref):
    r = r_ref[...]
    o_ref[...] = h_ref[...] + r[:, 2:3] * y0_ref[...] + r[:, 3:4] * y1_ref[...]


def _moe_combine(h2, route, yg, *, tm=512):
    T, D = h2.shape
    nt = T // tm
    return pl.pallas_call(
        _moe_combine_kernel,
        grid=(nt,),
        in_specs=[pl.BlockSpec((tm, D), lambda i: (i, 0)),
                  pl.BlockSpec((tm, LANES), lambda i: (i, 0)),
                  pl.BlockSpec((tm, D), lambda i: (i, 0)),
                  pl.BlockSpec((tm, D), lambda i: (i + nt, 0))],
        out_specs=pl.BlockSpec((tm, D), lambda i: (i, 0)),
        out_shape=jax.ShapeDtypeStruct((T, D), F32),
        compiler_params=_cparams(("parallel",)),
        name="moe_combine",
    )(h2, route, yg, yg)


def _moe_ffn(h2, norm_gain, w_router, b_router, we_gate, we_up, we_down):
    T, D = h2.shape
    E = we_gate.shape[0]
    tm = MOE_TILE
    route = _router(h2, norm_gain, w_router, b_router)
    expert = route[:, 0:2].astype(jnp.int32).T.reshape(-1)
    onehot = (expert[:, None] == jnp.arange(E)[None, :]).astype(jnp.int32)
    rank = jnp.sum((jnp.cumsum(onehot, axis=0) - onehot) * onehot, axis=1)
    tiles = (jnp.sum(onehot, axis=0) + tm - 1) // tm
    tile_end = jnp.cumsum(tiles)
    pos = ((tile_end - tiles) * tm)[expert] + rank
    n_rows = 2 * T + E * tm
    row_token = jnp.zeros((n_rows,), jnp.int32).at[pos].set(jnp.tile(jnp.arange(T, dtype=jnp.int32), 2))
    tile_expert = jnp.minimum(jnp.sum(jnp.arange(n_rows // tm)[:, None] >= tile_end[None, :], axis=1),
                              E - 1).astype(jnp.int32)
    xs = _gather_rows(h2, row_token)
    ys = _moe_experts(xs, norm_gain, tile_expert, tile_end[E - 1:].astype(jnp.int32), we_gate, we_up, we_down)
    return _moe_combine(h2, route, _gather_rows(ys, pos.astype(jnp.int32)))


def _rel_bucket(dist):
    n = jnp.maximum(dist, 0)
    max_exact = REL_BUCKETS // 2
    nf = jnp.maximum(n, max_exact).astype(F32)
    large = max_exact + (jnp.log(nf / max_exact) / math.log(REL_MAX_DIST / max_exact)
                         * (REL_BUCKETS - max_exact)).astype(jnp.int32)
    return jnp.where(n < max_exact, n, jnp.minimum(large, REL_BUCKETS - 1))


def _bias_tiles_kernel(tab_ref, idx_ref, o_ref):
    t = idx_ref.shape[-1]

    def chunk(c, carry):
        rows = pl.ds(pl.multiple_of(c * 8, 8), 8)
        idx = idx_ref[0, rows, :]
        for h in range(N_SCORE_HEADS):
            tile = jnp.zeros((8, t), F32)
            for b in range(REL_BUCKETS):
                tile = jnp.where(idx == b, tab_ref[h, b] * LOG2E, tile)
            o_ref[h, 0, rows, :] = tile
        return carry

    lax.fori_loop(0, t // 8, chunk, 0)


def _bias_tiles(rel_bias, S, t):
    n_delta = S // t
    i = jnp.arange(t)
    dist = (jnp.arange(n_delta) * t)[:, None, None] + i[None, None, :] - i[None, :, None]
    idx = _rel_bucket(dist).astype(jnp.int32)
    return pl.pallas_call(
        _bias_tiles_kernel,
        grid=(n_delta,),
        in_specs=[pl.BlockSpec(memory_space=pltpu.SMEM),
                  pl.BlockSpec((1, t, t), lambda d: (d, 0, 0))],
        out_specs=pl.BlockSpec((N_SCORE_HEADS, 1, t, t), lambda d: (0, d, 0, 0)),
        out_shape=jax.ShapeDtypeStruct((N_SCORE_HEADS, n_delta, t, t), F32),
        compiler_params=_cparams(("parallel",)),
        name="bias_tiles",
    )(rel_bias.T, idx)


def _flash_steps(mls, acc_refs, scores, masks, v_ts):
    new_mls, alphas, probs = [], [], []
    for (m, l), s, mask in zip(mls, scores, masks):
        if mask is not None:
            s = jnp.where(mask, s, NEG_INF)
        m_new = jnp.maximum(m, jnp.max(s, axis=0, keepdims=True))
        alpha = jnp.exp2(m - m_new)
        p = jnp.exp2(s - m_new)
        if mask is not None:
            p = jnp.where(mask, p, 0.0)
        new_mls.append((m_new, alpha * l + jnp.sum(p, axis=0, keepdims=True)))
        alphas.append(alpha)
        probs.append(p.astype(BF16))
    for acc_ref, alpha, p, v_t in zip(acc_refs, alphas, probs, v_ts):
        acc_ref[...] = alpha * acc_ref[...] + jnp.dot(v_t, p, preferred_element_type=F32)
    return tuple(new_mls)


def _flash_init(nq):
    return (jnp.full((1, nq), NEG_INF, F32), jnp.zeros((1, nq), F32))


def _flash_out(ml, acc_ref):
    return acc_ref[...] / jnp.maximum(ml[1], 1e-30)


def _key_tile(kb, t):
    return pl.ds(pl.multiple_of(kb * t, t), t)


def _causal_t(t):
    return lax.broadcasted_iota(jnp.int32, (t, t), 1) >= lax.broadcasted_iota(jnp.int32, (t, t), 0)


def _row_band(x, lo, hi):
    row = lax.broadcasted_iota(jnp.int32, (x.shape[0], 1), 0)
    return jnp.where((row >= lo) & (row < hi), x, jnp.zeros_like(x))


def _rank_select(score, cand_valid, topk):
    C = score.shape[0]
    row = lax.broadcasted_iota(jnp.int32, score.shape, 0)
    rank = jnp.zeros(score.shape, F32)
    for c in range(C):
        sc = score[c:c + 1, :]
        tie = (row > c).astype(F32)
        rank = rank + jnp.where(sc > score, 1.0, jnp.where(sc == score, tie, 0.0))
    return jnp.where(cand_valid & (rank < topk), 1.0, 0.0)


def _diff_attn_kernel(lam_ref, sub_ref, qt_ref, k_ref, vt_ref, bias_ref, o_ref, acc_ref, *, lambda_init):
    t = ATT_TILE
    nb = qt_ref.shape[0]
    qb = pl.program_id(2)
    chains = [(b, mi) for b in range(nb) for mi in range(2)]
    qm = [_row_band(qt_ref[b], mi * HEAD_DIM, (mi + 1) * HEAD_DIM) for b, mi in chains]
    acc_refs = [acc_ref.at[c] for c in range(len(chains))]
    acc_ref[...] = jnp.zeros(acc_ref.shape, F32)

    def step(kb, mls, mask):
        ks = [k_ref[b, _key_tile(kb, t), :] for b in range(nb)]
        v_ts = [vt_ref[b, :, _key_tile(kb, t)] for b in range(nb)]
        scores = [jnp.dot(ks[b], qm[c], preferred_element_type=F32) + bias_ref[mi, qb - kb]
                  for c, (b, mi) in enumerate(chains)]
        return _flash_steps(mls, acc_refs, scores, [mask] * len(chains), [v_ts[b] for b, _ in chains])

    mls = lax.fori_loop(0, qb, lambda kb, mls: step(kb, mls, None), tuple(_flash_init(t) for _ in chains))
    mls = step(qb, mls, _causal_t(t))
    lv = lam_ref[...]
    lam = (jnp.exp(jnp.sum(lv[0:1] * lv[1:2], axis=-1, keepdims=True))
           - jnp.exp(jnp.sum(lv[2:3] * lv[3:4], axis=-1, keepdims=True)) + lambda_init)
    for b in range(nb):
        o = _flash_out(mls[2 * b], acc_refs[2 * b]) - lam * _flash_out(mls[2 * b + 1], acc_refs[2 * b + 1])
        r = lax.rsqrt(jnp.mean(o * o, axis=0, keepdims=True) + NORM_EPS)
        o_ref[b] = (o * r * sub_ref[...] * (1.0 - lambda_init)).astype(o_ref.dtype)


def _diff_attention(q_t, qkv, v_t, lam4, subln_gain, bias_tiles, lambda_init):
    B, S, _ = qkv.shape
    t = ATT_TILE
    nq = S // t
    H = DIFF_HEADS
    nb = ATT_BATCH
    assert B % nb == 0
    return pl.pallas_call(
        functools.partial(_diff_attn_kernel, lambda_init=lambda_init),
        grid=(H, B // nb, nq),
        in_specs=[pl.BlockSpec((4, HEAD_DIM), lambda h, b, i: (0, 0)),
                  pl.BlockSpec((LANES, 1), lambda h, b, i: (0, 0)),
                  pl.BlockSpec((nb, LANES, t), lambda h, b, i: (b, h, i)),
                  pl.BlockSpec((nb, S, LANES), lambda h, b, i: (b, 0, H + h)),
                  pl.BlockSpec((nb, LANES, S), lambda h, b, i: (b, h, 0)),
                  pl.BlockSpec((2, nq, t, t), lambda h, b, i: (h, 0, 0, 0))],
        out_specs=pl.BlockSpec((nb, LANES, t), lambda h, b, i: (b, h, i)),
        out_shape=jax.ShapeDtypeStruct((B, H * LANES, S), BF16),
        scratch_shapes=[pltpu.VMEM((2 * nb, LANES, t), F32)],
        compiler_params=_cparams(("parallel", "parallel", "arbitrary")),
        name="diff_attention",
    )(lam4, subln_gain.reshape(LANES, 1), q_t, qkv, v_t, bias_tiles)


def _moba_kernel(qt_ref, k_ref, vt_ref, bias_ref, o_ref, kmean_ref, sel_ref, acc_ref):
    t = ATT_TILE
    nb = qt_ref.shape[0]
    qb = pl.program_id(2)
    n_blk = kmean_ref.shape[1]

    @pl.when(qb == 0)
    def _():
        for b in range(nb):
            for n in range(n_blk):
                kb = k_ref[b, n * t:(n + 1) * t, :].astype(F32)
                kmean_ref[b, n:n + 1, :] = jnp.mean(kb, axis=0, keepdims=True)

    chains = [(b, hj) for b in range(nb) for hj in range(2)]
    qm = [_row_band(qt_ref[b], hj * HEAD_DIM, (hj + 1) * HEAD_DIM) for b, hj in chains]
    acc_refs = [acc_ref.at[c] for c in range(len(chains))]
    past = lax.broadcasted_iota(jnp.int32, (n_blk, t), 0) < qb
    for c, (b, hj) in enumerate(chains):
        km = kmean_ref[b]
        km_hi = km.astype(BF16)
        km_lo = (km - km_hi.astype(F32)).astype(BF16)
        gate = (jnp.dot(km_hi, qm[c], preferred_element_type=F32)
                + jnp.dot(km_lo, qm[c], preferred_element_type=F32))
        sel_ref[c] = _rank_select(jnp.where(past, gate, NEG_INF), past, MOBA_TOPK)
    acc_ref[...] = jnp.zeros(acc_ref.shape, F32)

    def step(kb, mls, diagonal):
        ks = [k_ref[b, _key_tile(kb, t), :] for b in range(nb)]
        v_ts = [vt_ref[b, :, _key_tile(kb, t)] for b in range(nb)]
        scores = [jnp.dot(ks[b], qm[c], preferred_element_type=F32) + bias_ref[hj, qb - kb]
                  for c, (b, hj) in enumerate(chains)]
        masks = [_causal_t(t) if diagonal else sel_ref[c, pl.ds(kb, 1), :] > 0.5 for c in range(len(chains))]
        return _flash_steps(mls, acc_refs, scores, masks,
                            [v_ts[b][hj * HEAD_DIM:(hj + 1) * HEAD_DIM, :] for b, hj in chains])

    mls = lax.fori_loop(0, qb, lambda kb, mls: step(kb, mls, False), tuple(_flash_init(t) for _ in chains))
    mls = step(qb, mls, True)
    for c, (b, hj) in enumerate(chains):
        o_ref[b, hj * HEAD_DIM:(hj + 1) * HEAD_DIM, :] = _flash_out(mls[c], acc_refs[c]).astype(o_ref.dtype)


def _moba_attention(q_t, qkv, v_t, bias_tiles):
    B, S, _ = qkv.shape
    t = ATT_TILE
    assert t == MOBA_BLOCK and S % t == 0
    nq = S // t
    HP = D_MODEL // LANES
    nb = ATT_BATCH
    assert B % nb == 0
    return pl.pallas_call(
        _moba_kernel,
        grid=(HP, B // nb, nq),
        in_specs=[pl.BlockSpec((nb, LANES, t), lambda h, b, i: (b, h, i)),
                  pl.BlockSpec((nb, S, LANES), lambda h, b, i: (b, 0, HP + h)),
                  pl.BlockSpec((nb, LANES, S), lambda h, b, i: (b, h, 0)),
                  pl.BlockSpec((2, nq, t, t), lambda h, b, i: (h, 0, 0, 0))],
        out_specs=pl.BlockSpec((nb, LANES, t), lambda h, b, i: (b, h, i)),
        out_shape=jax.ShapeDtypeStruct((B, D_MODEL, S), BF16),
        scratch_shapes=[pltpu.VMEM((nb, S // t, LANES), F32),
                        pltpu.VMEM((2 * nb, S // t, t), F32),
                        pltpu.VMEM((2 * nb, HEAD_DIM, t), F32)],
        compiler_params=_cparams(("parallel", "parallel", "arbitrary")),
        name="moba_attention",
    )(q_t, qkv, v_t, bias_tiles)


def _compress_kernel(x_ref, pos_ref, w1_ref, w2_ref, gain_ref, o_ref, *, normalize):
    G, n_half, width = x_ref.shape[1:]
    x = x_ref[0].reshape(G * n_half, width).astype(F32)
    first = (x + pos_ref[0:1, :]).astype(BF16)
    second = (x + pos_ref[1:2, :]).astype(BF16)
    u = jnp.dot(first, w1_ref[0:width, :], preferred_element_type=F32)
    low = jnp.dot(second, w1_ref[width:2 * width, :], preferred_element_type=F32)
    rows = G * n_half
    hid = u + pltpu.roll(low, rows - 1, 0)
    c = jnp.dot(jax.nn.gelu(hid).astype(BF16), w2_ref[...], preferred_element_type=F32)
    if normalize:
        r = lax.rsqrt(jnp.mean(c * c, axis=-1, keepdims=True) + NORM_EPS)
        c = c * r * gain_ref[...]
    o_ref[0] = c.reshape(G, n_half, HEAD_DIM).astype(o_ref.dtype)


def _compress(x, pos, w1, w2, gain, normalize):
    B, G, S, d = x.shape
    n_half = S // NSA_CMP_STRIDE
    width = NSA_CMP_STRIDE * d
    return pl.pallas_call(
        functools.partial(_compress_kernel, normalize=normalize),
        grid=(B,),
        in_specs=[pl.BlockSpec((1, G, n_half, width), lambda b: (b, 0, 0, 0)),
                  pl.BlockSpec((2, width), lambda b: (0, 0)),
                  pl.BlockSpec((2 * width, NSA_CMP_HIDDEN), lambda b: (0, 0)),
                  pl.BlockSpec((NSA_CMP_HIDDEN, d), lambda b: (0, 0)),
                  pl.BlockSpec((1, d), lambda b: (0, 0))],
        out_specs=pl.BlockSpec((1, G, n_half, d), lambda b: (b, 0, 0, 0)),
        out_shape=jax.ShapeDtypeStruct((B, G, n_half, d), BF16),
        compiler_params=_cparams(("parallel",)),
        name="nsa_compress",
    )(x.reshape(B, G, n_half, width), pos.reshape(2, width), w1, w2, gain.reshape(1, d))


def _nsa_kernel(qt_ref, kc_ref, vct_ref, ks_ref, vst_ref, kw_ref, vwt_ref, g_ref, bias_ref,
                ov_ref, o_ref, sel_ref, acc_ref, ocmp_ref, osel_ref):
    t = ATT_TILE
    R = NSA_GROUP_SIZE
    d = HEAD_DIM
    qb = pl.program_id(2)
    n_cmp_rows = kc_ref.shape[2]
    n_sel = ov_ref.shape[0]
    qt = qt_ref[0]
    qh = [qt[r * d:(r + 1) * d, :] for r in range(R)]
    qpos = qb * t + lax.broadcasted_iota(jnp.int32, (1, t), 1)

    n_idx = lax.broadcasted_iota(jnp.int32, (n_cmp_rows, 1), 0)
    cmask = (n_idx * NSA_CMP_STRIDE + (NSA_CMP_BLOCK - 1) <= qpos) & (n_idx < n_cmp_rows - 1)
    kc = kc_ref[0, 0]
    psum = jnp.zeros((n_cmp_rows, t), F32)
    for r in range(R):
        cl = jnp.where(cmask, jnp.dot(kc, qh[r], preferred_element_type=F32), NEG_INF)
        cm = jnp.max(cl, axis=0, keepdims=True)
        cp = jnp.where(cmask, jnp.exp2(cl - cm), 0.0)
        cp = cp / jnp.maximum(jnp.sum(cp, axis=0, keepdims=True), 1e-30)
        ocmp_ref[r] = jnp.dot(vct_ref[0, 0], cp.astype(BF16), preferred_element_type=F32)
        psum = psum + cp

    p_hi = psum.astype(BF16)
    p_lo = (psum - p_hi.astype(F32)).astype(BF16)
    imp = (jnp.dot(ov_ref[...], p_hi, preferred_element_type=F32)
           + jnp.dot(ov_ref[...], p_lo, preferred_element_type=F32))
    blk = lax.broadcasted_iota(jnp.int32, (n_sel, 1), 0)
    cur = qpos // NSA_SEL_BLOCK
    forced = (blk == 0) | (blk == cur) | (blk == cur - 1)
    score = jnp.where(forced, FORCE_SCORE, jnp.where(blk <= cur, imp, NEG_INF))
    sel_ref[...] = _rank_select(score, blk <= cur, NSA_SEL_TOPK)

    causal = _causal_t(t)
    per_tile = t // NSA_SEL_BLOCK

    def sweep(k_ref, vt_ref, kb, delta, mls, mask):
        k = k_ref[0, 0, _key_tile(kb, t), :]
        v_t = vt_ref[0, :, _key_tile(kb, t)]
        scores = [jnp.dot(k, qh[r], preferred_element_type=F32) + bias_ref[r, delta] for r in range(R)]
        return _flash_steps(mls, [acc_ref.at[r] for r in range(R)], scores, [mask] * R, [v_t] * R)

    def chosen_rows(kb):
        rows = [jnp.broadcast_to(sel_ref[pl.ds(kb * per_tile + c, 1), :], (NSA_SEL_BLOCK, t))
                for c in range(per_tile)]
        return jnp.concatenate(rows, axis=0) > 0.5

    init = tuple(_flash_init(t) for _ in range(R))

    acc_ref[...] = jnp.zeros(acc_ref.shape, F32)
    mls = lax.fori_loop(0, qb, lambda kb, mls: sweep(ks_ref, vst_ref, kb, qb - kb, mls, chosen_rows(kb)), init)
    mls = sweep(ks_ref, vst_ref, qb, 0, mls, chosen_rows(qb) & causal)
    for r in range(R):
        osel_ref[r] = _flash_out(mls[r], acc_ref.at[r])

    acc_ref[...] = jnp.zeros(acc_ref.shape, F32)
    key_j = lax.broadcasted_iota(jnp.int32, (t, t), 0)
    qry_i = lax.broadcasted_iota(jnp.int32, (t, t), 1)
    mls = init
    for back in range(NSA_WINDOW // t, -1, -1):
        dist = back * t + qry_i - key_j
        wmask = (dist >= 0) & (dist < NSA_WINDOW) & (qb >= back)
        mls = sweep(kw_ref, vwt_ref, jnp.maximum(qb - back, 0), back, mls, wmask)

    gates = jax.nn.sigmoid(g_ref[0, 0].astype(F32))
    for r in range(R):
        o = (gates[3 * r:3 * r + 1, :] * ocmp_ref[r]
             + gates[3 * r + 1:3 * r + 2, :] * osel_ref[r]
             + gates[3 * r + 2:3 * r + 3, :] * _flash_out(mls[r], acc_ref.at[r]))
        o_ref[0, r * d:(r + 1) * d, :] = o.astype(o_ref.dtype)


def _nsa_overlap_t(S):
    n_cmp_rows = S // NSA_CMP_STRIDE
    n_sel = S // NSA_SEL_BLOCK
    cmp_start = jnp.arange(n_cmp_rows) * NSA_CMP_STRIDE
    sel_start = jnp.arange(n_sel) * NSA_SEL_BLOCK
    overlap = jnp.clip(jnp.minimum(cmp_start[None, :] + NSA_CMP_BLOCK, sel_start[:, None] + NSA_SEL_BLOCK)
                       - jnp.maximum(cmp_start[None, :], sel_start[:, None]), 0).astype(F32) / NSA_CMP_STRIDE
    return jnp.where(cmp_start[None, :] + NSA_CMP_BLOCK <= S, overlap, 0.0).astype(BF16)


def _nsa_attention(q_t, kc, vc_t, ks, vs_t, kw, vw_t, gates_t, bias_tiles):
    B, _, S = q_t.shape
    t = ATT_TILE
    nq = S // t
    G, R, d = NSA_KV_GROUPS, NSA_GROUP_SIZE, HEAD_DIM
    n_cmp_rows = kc.shape[2]
    overlap_t = _nsa_overlap_t(S)
    n_sel = overlap_t.shape[0]
    k_spec = pl.BlockSpec((1, 1, S, d), lambda g, b, i: (b, g, 0, 0))
    vt_spec = pl.BlockSpec((1, d, S), lambda g, b, i: (b, g, 0))
    return pl.pallas_call(
        _nsa_kernel,
        grid=(G, B, nq),
        in_specs=[pl.BlockSpec((1, R * d, t), lambda g, b, i: (b, g, i)),
                  pl.BlockSpec((1, 1, n_cmp_rows, d), lambda g, b, i: (b, g, 0, 0)),
                  pl.BlockSpec((1, 1, d, n_cmp_rows), lambda g, b, i: (b, g, 0, 0)),
                  k_spec, vt_spec, k_spec, vt_spec,
                  pl.BlockSpec((1, 1, 3 * R, t), lambda g, b, i: (b, g, 0, i)),
                  pl.BlockSpec((R, nq, t, t), lambda g, b, i: (g, 0, 0, 0)),
                  pl.BlockSpec((n_sel, n_cmp_rows), lambda g, b, i: (0, 0))],
        out_specs=pl.BlockSpec((1, R * d, t), lambda g, b, i: (b, g, i)),
        out_shape=jax.ShapeDtypeStruct((B, G * R * d, S), BF16),
        scratch_shapes=[pltpu.VMEM((n_sel, t), F32),
                        pltpu.VMEM((R, d, t), F32),
                        pltpu.VMEM((R, d, t), F32),
                        pltpu.VMEM((R, d, t), F32)],
        compiler_params=_cparams(("parallel", "parallel", "arbitrary")),
        name="nsa_attention",
    )(q_t, kc, vc_t, ks, vs_t, kw, vw_t, gates_t, bias_tiles, overlap_t)


def _qkv_head_gain(q_gain, k_gain, n_heads, n_plain):
    return jnp.concatenate([jnp.tile(q_gain, n_heads) * Q_SCALE, jnp.tile(k_gain, n_heads),
                            jnp.ones((n_plain,), F32)])


def _swap_last(x):
    return x.transpose(0, 2, 1)


def _diff_layer(h2, B, S, norm_mix, p, bias_tiles, lambda_init):
    w_in, w_out, q_gain, k_gain, lq1, lk1, lq2, lk2, subln = p
    hg = _qkv_head_gain(q_gain, k_gain, 2 * DIFF_HEADS, D_MODEL)
    qkv = _norm_proj(h2, norm_mix, w_in.astype(BF16), hg, 2 * D_MODEL).reshape(B, S, -1)
    o_t = _diff_attention(_swap_last(qkv[:, :, :D_MODEL]), qkv, _swap_last(qkv[:, :, 2 * D_MODEL:]),
                          jnp.stack([lq1, lk1, lq2, lk2]), subln, bias_tiles, lambda_init)
    return _proj_residual(_swap_last(o_t).reshape(B * S, -1), w_out.astype(BF16), h2)


def _moba_layer(h2, B, S, norm_mix, p, bias_tiles):
    w_in, w_out, q_gain, k_gain = p
    hg = _qkv_head_gain(q_gain, k_gain, N_SCORE_HEADS, D_MODEL)
    qkv = _norm_proj(h2, norm_mix, w_in.astype(BF16), hg, 2 * D_MODEL).reshape(B, S, -1)
    o_t = _moba_attention(_swap_last(qkv[:, :, :D_MODEL]), qkv, _swap_last(qkv[:, :, 2 * D_MODEL:]), bias_tiles)
    return _proj_residual(_swap_last(o_t).reshape(B * S, -1), w_out.astype(BF16), h2)


def _nsa_layer(h2, B, S, norm_mix, p, bias_tiles):
    w_in, w_out, q_gain, k_gain, pos_k, pos_v, k_w1, k_w2, v_w1, v_w2 = p
    G, d = NSA_KV_GROUPS, HEAD_DIM
    kvw = G * d
    qw = N_SCORE_HEADS * d
    sec = {name: slice(qw + i * kvw, qw + (i + 1) * kvw) for i, name in
           enumerate(("kc", "vc", "ks", "vs", "kw", "vw"))}
    n_gate = 3 * N_SCORE_HEADS
    tn = 384
    n_used = qw + 6 * kvw + n_gate
    n_pad = -n_used % tn
    w_perm = jnp.concatenate([w_in[:, :qw], w_in[:, sec["ks"]], w_in[:, sec["kw"]], w_in[:, sec["kc"]],
                              w_in[:, sec["vc"]], w_in[:, sec["vs"]], w_in[:, sec["vw"]],
                              w_in[:, qw + 6 * kvw:], jnp.zeros((D_MODEL, n_pad), F32)], axis=1)
    hg = jnp.concatenate([jnp.tile(q_gain, N_SCORE_HEADS) * Q_SCALE, jnp.tile(k_gain[1], G),
                          jnp.tile(k_gain[2], G), jnp.ones((n_used + n_pad - qw - 2 * kvw,), F32)])
    proj = _norm_proj(h2, norm_mix, w_perm.astype(BF16), hg, qw + 2 * kvw, tn=tn).reshape(B, S, -1)

    def section(i):
        return proj[:, :, qw + i * kvw:qw + (i + 1) * kvw]

    def group_major(x):
        return x.reshape(B, S, G, d).transpose(0, 2, 1, 3)

    ks, kw, kc, vc, vs, vw = (section(i) for i in range(6))
    gates_t = _swap_last(proj[:, :, qw + 6 * kvw:n_used]).reshape(B, G, n_gate // G, S)
    k_cmp = _compress(group_major(kc), pos_k, k_w1.astype(BF16), k_w2.astype(BF16), k_gain[0], True)
    v_cmp = _compress(group_major(vc), pos_v, v_w1.astype(BF16), v_w2.astype(BF16), k_gain[0], False)
    o_t = _nsa_attention(_swap_last(proj[:, :, :qw]), k_cmp, v_cmp.transpose(0, 1, 3, 2),
                         group_major(ks), _swap_last(vs), group_major(kw), _swap_last(vw), gates_t, bias_tiles)
    return _proj_residual(_swap_last(o_t).reshape(B * S, qw), w_out.astype(BF16), h2)


def _diff_lambda_init(layer):
    return 0.8 - 0.6 * math.exp(-0.3 * layer)


def kernel(x, rel_bias,
           l0_norm_mix, l0_w_in, l0_w_out, l0_q_gain, l0_k_gain, l0_lam_q1, l0_lam_k1, l0_lam_q2, l0_lam_k2,
           l0_subln_gain, l0_norm_ffn, l0_w_gate, l0_w_up, l0_w_down,
           l1_norm_mix, l1_w_in, l1_w_out, l1_q_gain, l1_k_gain, l1_norm_ffn, l1_w_router, l1_b_router,
           l1_we_gate, l1_we_up, l1_we_down,
           l2_norm_mix, l2_w_in, l2_w_out, l2_q_gain, l2_k_gain, l2_cmp_pos_k, l2_cmp_pos_v, l2_cmp_k_w1,
           l2_cmp_k_w2, l2_cmp_v_w1, l2_cmp_v_w2, l2_norm_ffn, l2_w_gate, l2_w_up, l2_w_down,
           l3_norm_mix, l3_w_in, l3_w_out, l3_q_gain, l3_k_gain, l3_lam_q1, l3_lam_k1, l3_lam_q2, l3_lam_k2,
           l3_subln_gain, l3_norm_ffn, l3_w_router, l3_b_router, l3_we_gate, l3_we_up, l3_we_down):
    B, S, D = x.shape
    bias_tiles = _bias_tiles(rel_bias, S, ATT_TILE)
    h = x.reshape(B * S, D)

    h = _diff_layer(h, B, S, l0_norm_mix, (l0_w_in, l0_w_out, l0_q_gain, l0_k_gain, l0_lam_q1, l0_lam_k1,
                                           l0_lam_q2, l0_lam_k2, l0_subln_gain), bias_tiles, _diff_lambda_init(0))
    h = _dense_ffn(h, l0_norm_ffn, l0_w_gate.astype(BF16), l0_w_up.astype(BF16), l0_w_down.astype(BF16))

    h = _moba_layer(h, B, S, l1_norm_mix, (l1_w_in, l1_w_out, l1_q_gain, l1_k_gain), bias_tiles)
    h = _moe_ffn(h, l1_norm_ffn, l1_w_router, l1_b_router, l1_we_gate.astype(BF16), l1_we_up.astype(BF16),
                 l1_we_down.astype(BF16))

    h = _nsa_layer(h, B, S, l2_norm_mix, (l2_w_in, l2_w_out, l2_q_gain, l2_k_gain, l2_cmp_pos_k, l2_cmp_pos_v,
                                          l2_cmp_k_w1, l2_cmp_k_w2, l2_cmp_v_w1, l2_cmp_v_w2), bias_tiles)
    h = _dense_ffn(h, l2_norm_ffn, l2_w_gate.astype(BF16), l2_w_up.astype(BF16), l2_w_down.astype(BF16))

    h = _diff_layer(h, B, S, l3_norm_mix, (l3_w_in, l3_w_out, l3_q_gain, l3_k_gain, l3_lam_q1, l3_lam_k1,
                                           l3_lam_q2, l3_lam_k2, l3_subln_gain), bias_tiles, _diff_lambda_init(3))
    h = _moe_ffn(h, l3_norm_ffn, l3_w_router, l3_b_router, l3_we_gate.astype(BF16), l3_we_up.astype(BF16),
                 l3_we_down.astype(BF16))
    return h.reshape(B, S, D)
```

```python
import functools
import math

import jax
import jax.numpy as jnp
from jax import lax
from jax.experimental import pallas as pl
from jax.experimental.pallas import tpu as pltpu

F32 = jnp.float32
BF16 = jnp.bfloat16

D_MODEL = 1024
HEAD_DIM = 64
N_SCORE_HEADS = 16
DIFF_HEADS = 8
MOBA_BLOCK = 256
MOBA_TOPK = 3
NSA_KV_GROUPS = 4
NSA_GROUP_SIZE = 4
NSA_CMP_BLOCK = 32
NSA_CMP_STRIDE = 16
NSA_CMP_HIDDEN = 256
NSA_SEL_BLOCK = 64
NSA_SEL_TOPK = 16
NSA_WINDOW = 512
REL_BUCKETS = 32
REL_MAX_DIST = 1024
FFN_DIM = 3584
N_EXPERTS = 8
NORM_EPS = 1e-6
NEG_INF = -1e30
FORCE_SCORE = 1e30
ATTN_SCALE = HEAD_DIM ** -0.5
LOG2E = math.log2(math.e)
Q_SCALE = ATTN_SCALE * LOG2E

LANES = 128
ATT_TILE = 256
ATT_BATCH = 2
VMEM_LIMIT = 56 * 1024 * 1024


def _cparams(sem):
    return pltpu.CompilerParams(dimension_semantics=sem, vmem_limit_bytes=VMEM_LIMIT)


def _dot_nt(a, b):
    return lax.dot_general(a, b, (((1,), (1,)), ((), ())), preferred_element_type=F32)


def _norm_proj_kernel(x_ref, g_ref, w_ref, hg_ref, gm_ref, o_ref, xn_ref, *, n_norm_tiles):
    j = pl.program_id(1)

    @pl.when(j == 0)
    def _():
        x = x_ref[...]
        r = lax.rsqrt(jnp.mean(x * x, axis=-1, keepdims=True) + NORM_EPS)
        xn_ref[...] = (x * r * g_ref[...]).astype(BF16)

    acc = jnp.dot(xn_ref[...], w_ref[...], preferred_element_type=F32)

    @pl.when(j < n_norm_tiles)
    def _():
        sq = acc * acc
        hi = sq.astype(BF16)
        lo = (sq - hi.astype(F32)).astype(BF16)
        ms = (jnp.dot(hi, gm_ref[...], preferred_element_type=F32)
              + jnp.dot(lo, gm_ref[...], preferred_element_type=F32))
        o_ref[...] = (acc * lax.rsqrt(ms + NORM_EPS) * hg_ref[...]).astype(o_ref.dtype)

    @pl.when(j >= n_norm_tiles)
    def _():
        o_ref[...] = acc.astype(o_ref.dtype)


def _norm_proj(h2, norm_gain, w, head_gain, n_norm_cols, *, tm=512, tn=512):
    T, D = h2.shape
    N = w.shape[1]
    assert T % tm == 0 and N % tn == 0 and n_norm_cols % tn == 0 and tn % HEAD_DIM == 0
    col = jnp.arange(tn) // HEAD_DIM
    group_mean = jnp.where(col[:, None] == col[None, :], 1.0 / HEAD_DIM, 0.0).astype(BF16)
    return pl.pallas_call(
        functools.partial(_norm_proj_kernel, n_norm_tiles=n_norm_cols // tn),
        grid=(T // tm, N // tn),
        in_specs=[pl.BlockSpec((tm, D), lambda i, j: (i, 0)),
                  pl.BlockSpec((1, D), lambda i, j: (0, 0)),
                  pl.BlockSpec((D, tn), lambda i, j: (0, j)),
                  pl.BlockSpec((1, tn), lambda i, j: (0, j)),
                  pl.BlockSpec((tn, tn), lambda i, j: (0, 0))],
        out_specs=pl.BlockSpec((tm, tn), lambda i, j: (i, j)),
        out_shape=jax.ShapeDtypeStruct((T, N), BF16),
        scratch_shapes=[pltpu.VMEM((tm, D), BF16)],
        compiler_params=_cparams(("parallel", "arbitrary")),
        name="norm_proj",
    )(h2, norm_gain.reshape(1, D), w, head_gain.reshape(1, N), group_mean)


def _proj_res_kernel(a_ref, w_ref, r_ref, o_ref):
    o_ref[...] = r_ref[...] + jnp.dot(a_ref[...], w_ref[...], preferred_element_type=F32)


def _proj_residual(a, w, res, *, tm=512, tn=512):
    T, K = a.shape
    N = w.shape[1]
    return pl.pallas_call(
        _proj_res_kernel,
        grid=(T // tm, N // tn),
        in_specs=[pl.BlockSpec((tm, K), lambda i, j: (i, 0)),
                  pl.BlockSpec((K, tn), lambda i, j: (0, j)),
                  pl.BlockSpec((tm, tn), lambda i, j: (i, j))],
        out_specs=pl.BlockSpec((tm, tn), lambda i, j: (i, j)),
        out_shape=jax.ShapeDtypeStruct((T, N), F32),
        compiler_params=_cparams(("parallel", "arbitrary")),
        name="proj_residual",
    )(a, w, res)


def _ffn_kernel(h_ref, g_ref, wg_ref, wu_ref, wd_ref, o_ref, hn_ref):
    f = pl.program_id(1)

    @pl.when(f == 0)
    def _():
        x = h_ref[...]
        r = lax.rsqrt(jnp.mean(x * x, axis=-1, keepdims=True) + NORM_EPS)
        hn_ref[...] = (x * r * g_ref[...]).astype(BF16)
        o_ref[...] = x

    hn = hn_ref[...]
    gate = jnp.dot(hn, wg_ref[...], preferred_element_type=F32)
    up = jnp.dot(hn, wu_ref[...], preferred_element_type=F32)
    act = (gate * jax.nn.sigmoid(gate) * up).astype(BF16)
    o_ref[...] += jnp.dot(act, wd_ref[...], preferred_element_type=F32)


def _dense_ffn(h2, norm_gain, wg, wu, wd, *, tm=1024, tf=512):
    T, D = h2.shape
    F = wg.shape[1]
    return pl.pallas_call(
        _ffn_kernel,
        grid=(T // tm, F // tf),
        in_specs=[pl.BlockSpec((tm, D), lambda i, f: (i, 0)),
                  pl.BlockSpec((1, D), lambda i, f: (0, 0)),
                  pl.BlockSpec((D, tf), lambda i, f: (0, f)),
                  pl.BlockSpec((D, tf), lambda i, f: (0, f)),
                  pl.BlockSpec((tf, D), lambda i, f: (f, 0))],
        out_specs=pl.BlockSpec((tm, D), lambda i, f: (i, 0)),
        out_shape=jax.ShapeDtypeStruct((T, D), F32),
        scratch_shapes=[pltpu.VMEM((tm, D), BF16)],
        compiler_params=_cparams(("parallel", "arbitrary")),
        name="dense_ffn",
    )(h2, norm_gain.reshape(1, D), wg, wu, wd)


def _router_kernel(h_ref, g_ref, wr_ref, br_ref, gate_ref):
    x = h_ref[...]
    r = lax.rsqrt(jnp.mean(x * x, axis=-1, keepdims=True) + NORM_EPS)
    hn = x * r * g_ref[...]
    logits = jnp.dot(hn, wr_ref[...], preferred_element_type=F32,
                     precision=lax.Precision.HIGHEST) + br_ref[...]
    lane = lax.broadcasted_iota(jnp.int32, logits.shape, 1)
    logits = jnp.where(lane < N_EXPERTS, logits, NEG_INF)
    m1 = jnp.max(logits, axis=-1, keepdims=True)
    i1 = jnp.min(jnp.where(logits == m1, lane, LANES), axis=-1, keepdims=True)
    rest = jnp.where(lane == i1, NEG_INF, logits)
    m2 = jnp.max(rest, axis=-1, keepdims=True)
    i2 = jnp.min(jnp.where(rest == m2, lane, LANES), axis=-1, keepdims=True)
    e2 = jnp.exp(m2 - m1)
    w1 = 1.0 / (1.0 + e2)
    w2 = e2 / (1.0 + e2)
    gate_ref[...] = (jnp.where(lane == 0, i1.astype(F32), 0.0) + jnp.where(lane == 1, i2.astype(F32), 0.0)
                     + jnp.where(lane == 2, w1, 0.0) + jnp.where(lane == 3, w2, 0.0))


def _router(h2, norm_gain, w_router, b_router, *, tm=512):
    T, D = h2.shape
    wr = jnp.zeros((D, LANES), F32).at[:, :N_EXPERTS].set(w_router)
    br = jnp.zeros((1, LANES), F32).at[0, :N_EXPERTS].set(b_router)
    return pl.pallas_call(
        _router_kernel,
        grid=(T // tm,),
        in_specs=[pl.BlockSpec((tm, D), lambda i: (i, 0)),
                  pl.BlockSpec((1, D), lambda i: (0, 0)),
                  pl.BlockSpec((D, LANES), lambda i: (0, 0)),
                  pl.BlockSpec((1, LANES), lambda i: (0, 0))],
        out_specs=pl.BlockSpec((tm, LANES), lambda i: (i, 0)),
        out_shape=jax.ShapeDtypeStruct((T, LANES), F32),
        compiler_params=_cparams(("parallel",)),
        name="moe_router",
    )(h2, norm_gain.reshape(1, D), wr, br)


MOE_TILE = 512


def _gather_rows_kernel(idx_hbm, src_hbm, o_ref, idx_smem, idx_sem, row_sem):
    i = pl.program_id(0)
    n = o_ref.shape[0]
    idx_copy = pltpu.make_async_copy(idx_hbm.at[i], idx_smem, idx_sem)
    idx_copy.start()
    idx_copy.wait()

    def row_copy(r, src_row):
        return pltpu.make_async_copy(src_hbm.at[pl.ds(src_row, 1), :], o_ref.at[pl.ds(r, 1), :], row_sem)

    def issue(r, carry):
        row_copy(r, idx_smem[r]).start()
        return carry

    def drain(r, carry):
        row_copy(r, 0).wait()
        return carry

    lax.fori_loop(0, n, issue, 0, unroll=8)
    lax.fori_loop(0, n, drain, 0, unroll=8)


def _gather_rows(src, idx):
    M = idx.shape[0]
    D = src.shape[1]
    n = MOE_TILE
    assert M % n == 0
    return pl.pallas_call(
        _gather_rows_kernel,
        grid=(M // n,),
        in_specs=[pl.BlockSpec(memory_space=pl.ANY), pl.BlockSpec(memory_space=pl.ANY)],
        out_specs=pl.BlockSpec((n, D), lambda i: (i, 0)),
        out_shape=jax.ShapeDtypeStruct((M, D), src.dtype),
        scratch_shapes=[pltpu.SMEM((n,), jnp.int32), pltpu.SemaphoreType.DMA, pltpu.SemaphoreType.DMA],
        compiler_params=_cparams(("arbitrary",)),
        name="gather_rows",
    )(idx.reshape(M // n, n), src)


def _moe_expert_kernel(te_ref, nu_ref, x_ref, g_ref, wg_ref, wu_ref, wd_ref, o_ref, hn_ref):
    i = pl.program_id(0)
    f = pl.program_id(1)

    @pl.when(f == 0)
    def _():
        x = x_ref[...]
        r = lax.rsqrt(jnp.mean(x * x, axis=-1, keepdims=True) + NORM_EPS)
        hn_ref[...] = (x * r * g_ref[...]).astype(BF16)
        o_ref[...] = jnp.zeros(o_ref.shape, F32)

    @pl.when(i < nu_ref[0])
    def _():
        hn = hn_ref[...]
        gate = jnp.dot(hn, wg_ref[0], preferred_element_type=F32)
        up = jnp.dot(hn, wu_ref[0], preferred_element_type=F32)
        act = (gate * jax.nn.sigmoid(gate) * up).astype(BF16)
        o_ref[...] += jnp.dot(act, wd_ref[0], preferred_element_type=F32)


def _moe_experts(xs, norm_gain, tile_expert, n_used, we_gate, we_up, we_down, *, tf=512):
    N, D = xs.shape
    F = we_gate.shape[2]
    tm = MOE_TILE
    grid_spec = pltpu.PrefetchScalarGridSpec(
        num_scalar_prefetch=2,
        grid=(N // tm, F // tf),
        in_specs=[pl.BlockSpec((tm, D), lambda i, f, te, nu: (i, 0)),
                  pl.BlockSpec((1, D), lambda i, f, te, nu: (0, 0)),
                  pl.BlockSpec((1, D, tf), lambda i, f, te, nu: (te[i], 0, f)),
                  pl.BlockSpec((1, D, tf), lambda i, f, te, nu: (te[i], 0, f)),
                  pl.BlockSpec((1, tf, D), lambda i, f, te, nu: (te[i], f, 0))],
        out_specs=pl.BlockSpec((tm, D), lambda i, f, te, nu: (i, 0)),
        scratch_shapes=[pltpu.VMEM((tm, D), BF16)])
    return pl.pallas_call(
        _moe_expert_kernel,
        grid_spec=grid_spec,
        out_shape=jax.ShapeDtypeStruct((N, D), F32),
        compiler_params=_cparams(("arbitrary", "arbitrary")),
        name="moe_experts",
    )(tile_expert, n_used, xs, norm_gain.reshape(1, D), we_gate, we_up, we_down)


def _moe_combine_kernel(h_ref, r_ref, y0_ref, y1_ref, o_ref):
    r = r_ref[...]
    o_ref[...] = h_ref[...] + r[:, 2:3] * y0_ref[...] + r[:, 3:4] * y1_ref[...]


def _moe_combine(h2, route, yg, *, tm=512):
    T, D = h2.shape
    nt = T // tm
    return pl.pallas_call(
        _moe_combine_kernel,
        grid=(nt,),
        in_specs=[pl.BlockSpec((tm, D), lambda i: (i, 0)),
                  pl.BlockSpec((tm, LANES), lambda i: (i, 0)),
                  pl.BlockSpec((tm, D), lambda i: (i, 0)),
                  pl.BlockSpec((tm, D), lambda i: (i + nt, 0))],
        out_specs=pl.BlockSpec((tm, D), lambda i: (i, 0)),
        out_shape=jax.ShapeDtypeStruct((T, D), F32),
        compiler_params=_cparams(("parallel",)),
        name="moe_combine",
    )(h2, route, yg, yg)


def _moe_ffn(h2, norm_gain, w_router, b_router, we_gate, we_up, we_down):
    T, D = h2.shape
    E = we_gate.shape[0]
    tm = MOE_TILE
    route = _router(h2, norm_gain, w_router, b_router)
    expert = route[:, 0:2].astype(jnp.int32).T.reshape(-1)
    onehot = (expert[:, None] == jnp.arange(E)[None, :]).astype(jnp.int32)
    rank = jnp.sum((jnp.cumsum(onehot, axis=0) - onehot) * onehot, axis=1)
    tiles = (jnp.sum(onehot, axis=0) + tm - 1) // tm
    tile_end = jnp.cumsum(tiles)
    pos = ((tile_end - tiles) * tm)[expert] + rank
    n_rows = 2 * T + E * tm
    row_token = jnp.zeros((n_rows,), jnp.int32).at[pos].set(jnp.tile(jnp.arange(T, dtype=jnp.int32), 2))
    tile_expert = jnp.minimum(jnp.sum(jnp.arange(n_rows // tm)[:, None] >= tile_end[None, :], axis=1),
                              E - 1).astype(jnp.int32)
    xs = _gather_rows(h2, row_token)
    ys = _moe_experts(xs, norm_gain, tile_expert, tile_end[E - 1:].astype(jnp.int32), we_gate, we_up, we_down)
    return _moe_combine(h2, route, _gather_rows(ys, pos.astype(jnp.int32)))


def _rel_bucket(dist):
    n = jnp.maximum(dist, 0)
    max_exact = REL_BUCKETS // 2
    nf = jnp.maximum(n, max_exact).astype(F32)
    large = max_exact + (jnp.log(nf / max_exact) / math.log(REL_MAX_DIST / max_exact)
                         * (REL_BUCKETS - max_exact)).astype(jnp.int32)
    return jnp.where(n < max_exact, n, jnp.minimum(large, REL_BUCKETS - 1))


def _bias_tiles_kernel(tab_ref, idx_ref, o_ref):
    t = idx_ref.shape[-1]

    def chunk(c, carry):
        rows = pl.ds(pl.multiple_of(c * 8, 8), 8)
        idx = idx_ref[0, rows, :]
        for h in range(N_SCORE_HEADS):
            tile = jnp.zeros((8, t), F32)
            for b in range(REL_BUCKETS):
                tile = jnp.where(idx == b, tab_ref[h, b] * LOG2E, tile)
            o_ref[h, 0, rows, :] = tile
        return carry

    lax.fori_loop(0, t // 8, chunk, 0)


def _bias_tiles(rel_bias, S, t):
    n_delta = S // t
    i = jnp.arange(t)
    dist = (jnp.arange(n_delta) * t)[:, None, None] + i[None, None, :] - i[None, :, None]
    idx = _rel_bucket(dist).astype(jnp.int32)
    return pl.pallas_call(
        _bias_tiles_kernel,
        grid=(n_delta,),
        in_specs=[pl.BlockSpec(memory_space=pltpu.SMEM),
                  pl.BlockSpec((1, t, t), lambda d: (d, 0, 0))],
        out_specs=pl.BlockSpec((N_SCORE_HEADS, 1, t, t), lambda d: (0, d, 0, 0)),
        out_shape=jax.ShapeDtypeStruct((N_SCORE_HEADS, n_delta, t, t), F32),
        compiler_params=_cparams(("parallel",)),
        name="bias_tiles",
    )(rel_bias.T, idx)


def _flash_steps(mls, acc_refs, scores, masks, v_ts):
    new_mls, alphas, probs = [], [], []
    for (m, l), s, mask in zip(mls, scores, masks):
        if mask is not None:
            s = jnp.where(mask, s, NEG_INF)
        m_new = jnp.maximum(m, jnp.max(s, axis=0, keepdims=True))
        alpha = jnp.exp2(m - m_new)
        p = jnp.exp2(s - m_new)
        if mask is not None:
            p = jnp.where(mask, p, 0.0)
        new_mls.append((m_new, alpha * l + jnp.sum(p, axis=0, keepdims=True)))
        alphas.append(alpha)
        probs.append(p.astype(BF16))
    for acc_ref, alpha, p, v_t in zip(acc_refs, alphas, probs, v_ts):
        acc_ref[...] = alpha * acc_ref[...] + jnp.dot(v_t, p, preferred_element_type=F32)
    return tuple(new_mls)


def _flash_init(nq):
    return (jnp.full((1, nq), NEG_INF, F32), jnp.zeros((1, nq), F32))


def _flash_out(ml, acc_ref):
    return acc_ref[...] / jnp.maximum(ml[1], 1e-30)


def _key_tile(kb, t):
    return pl.ds(pl.multiple_of(kb * t, t), t)


def _causal_t(t):
    return lax.broadcasted_iota(jnp.int32, (t, t), 1) >= lax.broadcasted_iota(jnp.int32, (t, t), 0)


def _row_band(x, lo, hi):
    row = lax.broadcasted_iota(jnp.int32, (x.shape[0], 1), 0)
    return jnp.where((row >= lo) & (row < hi), x, jnp.zeros_like(x))


def _rank_select(score, cand_valid, topk):
    C = score.shape[0]
    row = lax.broadcasted_iota(jnp.int32, score.shape, 0)
    rank = jnp.zeros(score.shape, F32)
    for c in range(C):
        sc = score[c:c + 1, :]
        tie = (row > c).astype(F32)
        rank = rank + jnp.where(sc > score, 1.0, jnp.where(sc == score, tie, 0.0))
    return jnp.where(cand_valid & (rank < topk), 1.0, 0.0)


def _diff_attn_kernel(lam_ref, sub_ref, qt_ref, k_ref, vt_ref, bias_ref, o_ref, acc_ref, *, lambda_init):
    t = ATT_TILE
    nb = qt_ref.shape[0]
    qb = pl.program_id(2)
    chains = [(b, mi) for b in range(nb) for mi in range(2)]
    qm = [_row_band(qt_ref[b], mi * HEAD_DIM, (mi + 1) * HEAD_DIM) for b, mi in chains]
    acc_refs = [acc_ref.at[c] for c in range(len(chains))]
    acc_ref[...] = jnp.zeros(acc_ref.shape, F32)

    def step(kb, mls, mask):
        ks = [k_ref[b, _key_tile(kb, t), :] for b in range(nb)]
        v_ts = [vt_ref[b, :, _key_tile(kb, t)] for b in range(nb)]
        scores = [jnp.dot(ks[b], qm[c], preferred_element_type=F32) + bias_ref[mi, qb - kb]
                  for c, (b, mi) in enumerate(chains)]
        return _flash_steps(mls, acc_refs, scores, [mask] * len(chains), [v_ts[b] for b, _ in chains])

    mls = lax.fori_loop(0, qb, lambda kb, mls: step(kb, mls, None), tuple(_flash_init(t) for _ in chains))
    mls = step(qb, mls, _causal_t(t))
    lv = lam_ref[...]
    lam = (jnp.exp(jnp.sum(lv[0:1] * lv[1:2], axis=-1, keepdims=True))
           - jnp.exp(jnp.sum(lv[2:3] * lv[3:4], axis=-1, keepdims=True)) + lambda_init)
    for b in range(nb):
        o = _flash_out(mls[2 * b], acc_refs[2 * b]) - lam * _flash_out(mls[2 * b + 1], acc_refs[2 * b + 1])
        r = lax.rsqrt(jnp.mean(o * o, axis=0, keepdims=True) + NORM_EPS)
        o_ref[b] = (o * r * sub_ref[...] * (1.0 - lambda_init)).astype(o_ref.dtype)


def _diff_attention(q_t, qkv, v_t, lam4, subln_gain, bias_tiles, lambda_init):
    B, S, _ = qkv.shape
    t = ATT_TILE
    nq = S // t
    H = DIFF_HEADS
    nb = ATT_BATCH
    assert B % nb == 0
    return pl.pallas_call(
        functools.partial(_diff_attn_kernel, lambda_init=lambda_init),
        grid=(H, B // nb, nq),
        in_specs=[pl.BlockSpec((4, HEAD_DIM), lambda h, b, i: (0, 0)),
                  pl.BlockSpec((LANES, 1), lambda h, b, i: (0, 0)),
                  pl.BlockSpec((nb, LANES, t), lambda h, b, i: (b, h, i)),
                  pl.BlockSpec((nb, S, LANES), lambda h, b, i: (b, 0, H + h)),
                  pl.BlockSpec((nb, LANES, S), lambda h, b, i: (b, h, 0)),
                  pl.BlockSpec((2, nq, t, t), lambda h, b, i: (h, 0, 0, 0))],
        out_specs=pl.BlockSpec((nb, LANES, t), lambda h, b, i: (b, h, i)),
        out_shape=jax.ShapeDtypeStruct((B, H * LANES, S), BF16),
        scratch_shapes=[pltpu.VMEM((2 * nb, LANES, t), F32)],
        compiler_params=_cparams(("parallel", "parallel", "arbitrary")),
        name="diff_attention",
    )(lam4, subln_gain.reshape(LANES, 1), q_t, qkv, v_t, bias_tiles)


def _moba_kernel(qt_ref, k_ref, vt_ref, bias_ref, o_ref, kmean_ref, sel_ref, acc_ref):
    t = ATT_TILE
    nb = qt_ref.shape[0]
    qb = pl.program_id(2)
    n_blk = kmean_ref.shape[1]

    @pl.when(qb == 0)
    def _():
        for b in range(nb):
            for n in range(n_blk):
                kb = k_ref[b, n * t:(n + 1) * t, :].astype(F32)
                kmean_ref[b, n:n + 1, :] = jnp.mean(kb, axis=0, keepdims=True)

    chains = [(b, hj) for b in range(nb) for hj in range(2)]
    qm = [_row_band(qt_ref[b], hj * HEAD_DIM, (hj + 1) * HEAD_DIM) for b, hj in chains]
    acc_refs = [acc_ref.at[c] for c in range(len(chains))]
    past = lax.broadcasted_iota(jnp.int32, (n_blk, t), 0) < qb
    for c, (b, hj) in enumerate(chains):
        km = kmean_ref[b]
        km_hi = km.astype(BF16)
        km_lo = (km - km_hi.astype(F32)).astype(BF16)
        gate = (jnp.dot(km_hi, qm[c], preferred_element_type=F32)
                + jnp.dot(km_lo, qm[c], preferred_element_type=F32))
        sel_ref[c] = _rank_select(jnp.where(past, gate, NEG_INF), past, MOBA_TOPK)
    acc_ref[...] = jnp.zeros(acc_ref.shape, F32)

    def step(kb, mls, diagonal):
        ks = [k_ref[b, _key_tile(kb, t), :] for b in range(nb)]
        v_ts = [vt_ref[b, :, _key_tile(kb, t)] for b in range(nb)]
        scores = [jnp.dot(ks[b], qm[c], preferred_element_type=F32) + bias_ref[hj, qb - kb]
                  for c, (b, hj) in enumerate(chains)]
        masks = [_causal_t(t) if diagonal else sel_ref[c, pl.ds(kb, 1), :] > 0.5 for c in range(len(chains))]
        return _flash_steps(mls, acc_refs, scores, masks,
                            [v_ts[b][hj * HEAD_DIM:(hj + 1) * HEAD_DIM, :] for b, hj in chains])

    mls = lax.fori_loop(0, qb, lambda kb, mls: step(kb, mls, False), tuple(_flash_init(t) for _ in chains))
    mls = step(qb, mls, True)
    for c, (b, hj) in enumerate(chains):
        o_ref[b, hj * HEAD_DIM:(hj + 1) * HEAD_DIM, :] = _flash_out(mls[c], acc_refs[c]).astype(o_ref.dtype)


def _moba_attention(q_t, qkv, v_t, bias_tiles):
    B, S, _ = qkv.shape
    t = ATT_TILE
    assert t == MOBA_BLOCK and S % t == 0
    nq = S // t
    HP = D_MODEL // LANES
    nb = ATT_BATCH
    assert B % nb == 0
    return pl.pallas_call(
        _moba_kernel,
        grid=(HP, B // nb, nq),
        in_specs=[pl.BlockSpec((nb, LANES, t), lambda h, b, i: (b, h, i)),
                  pl.BlockSpec((nb, S, LANES), lambda h, b, i: (b, 0, HP + h)),
                  pl.BlockSpec((nb, LANES, S), lambda h, b, i: (b, h, 0)),
                  pl.BlockSpec((2, nq, t, t), lambda h, b, i: (h, 0, 0, 0))],
        out_specs=pl.BlockSpec((nb, LANES, t), lambda h, b, i: (b, h, i)),
        out_shape=jax.ShapeDtypeStruct((B, D_MODEL, S), BF16),
        scratch_shapes=[pltpu.VMEM((nb, S // t, LANES), F32),
                        pltpu.VMEM((2 * nb, S // t, t), F32),
                        pltpu.VMEM((2 * nb, HEAD_DIM, t), F32)],
        compiler_params=_cparams(("parallel", "parallel", "arbitrary")),
        name="moba_attention",
    )(q_t, qkv, v_t, bias_tiles)


def _compress_kernel(x_ref, pos_ref, w1_ref, w2_ref, gain_ref, o_ref, *, normalize):
    G, n_half, width = x_ref.shape[1:]
    x = x_ref[0].reshape(G * n_half, width).astype(F32)
    first = (x + pos_ref[0:1, :]).astype(BF16)
    second = (x + pos_ref[1:2, :]).astype(BF16)
    u = jnp.dot(first, w1_ref[0:width, :], preferred_element_type=F32)
    low = jnp.dot(second, w1_ref[width:2 * width, :], preferred_element_type=F32)
    rows = G * n_half
    hid = u + pltpu.roll(low, rows - 1, 0)
    c = jnp.dot(jax.nn.gelu(hid).astype(BF16), w2_ref[...], preferred_element_type=F32)
    if normalize:
        r = lax.rsqrt(jnp.mean(c * c, axis=-1, keepdims=True) + NORM_EPS)
        c = c * r * gain_ref[...]
    o_ref[0] = c.reshape(G, n_half, HEAD_DIM).astype(o_ref.dtype)


def _compress(x, pos, w1, w2, gain, normalize):
    B, G, S, d = x.shape
    n_half = S // NSA_CMP_STRIDE
    width = NSA_CMP_STRIDE * d
    return pl.pallas_call(
        functools.partial(_compress_kernel, normalize=normalize),
        grid=(B,),
        in_specs=[pl.BlockSpec((1, G, n_half, width), lambda b: (b, 0, 0, 0)),
                  pl.BlockSpec((2, width), lambda b: (0, 0)),
                  pl.BlockSpec((2 * width, NSA_CMP_HIDDEN), lambda b: (0, 0)),
                  pl.BlockSpec((NSA_CMP_HIDDEN, d), lambda b: (0, 0)),
                  pl.BlockSpec((1, d), lambda b: (0, 0))],
        out_specs=pl.BlockSpec((1, G, n_half, d), lambda b: (b, 0, 0, 0)),
        out_shape=jax.ShapeDtypeStruct((B, G, n_half, d), BF16),
        compiler_params=_cparams(("parallel",)),
        name="nsa_compress",
    )(x.reshape(B, G, n_half, width), pos.reshape(2, width), w1, w2, gain.reshape(1, d))


def _nsa_kernel(qt_ref, kc_ref, vct_ref, ks_ref, vst_ref, kw_ref, vwt_ref, g_ref, bias_ref,
                ov_ref, o_ref, sel_ref, acc_ref, ocmp_ref, osel_ref):
    t = ATT_TILE
    R = NSA_GROUP_SIZE
    d = HEAD_DIM
    qb = pl.program_id(2)
    n_cmp_rows = kc_ref.shape[2]
    n_sel = ov_ref.shape[0]
    qt = qt_ref[0]
    qh = [qt[r * d:(r + 1) * d, :] for r in range(R)]
    qpos = qb * t + lax.broadcasted_iota(jnp.int32, (1, t), 1)

    n_idx = lax.broadcasted_iota(jnp.int32, (n_cmp_rows, 1), 0)
    cmask = (n_idx * NSA_CMP_STRIDE + (NSA_CMP_BLOCK - 1) <= qpos) & (n_idx < n_cmp_rows - 1)
    kc = kc_ref[0, 0]
    psum = jnp.zeros((n_cmp_rows, t), F32)
    for r in range(R):
        cl = jnp.where(cmask, jnp.dot(kc, qh[r], preferred_element_type=F32), NEG_INF)
        cm = jnp.max(cl, axis=0, keepdims=True)
        cp = jnp.where(cmask, jnp.exp2(cl - cm), 0.0)
        cp = cp / jnp.maximum(jnp.sum(cp, axis=0, keepdims=True), 1e-30)
        ocmp_ref[r] = jnp.dot(vct_ref[0, 0], cp.astype(BF16), preferred_element_type=F32)
        psum = psum + cp

    p_hi = psum.astype(BF16)
    p_lo = (psum - p_hi.astype(F32)).astype(BF16)
    imp = (jnp.dot(ov_ref[...], p_hi, preferred_element_type=F32)
           + jnp.dot(ov_ref[...], p_lo, preferred_element_type=F32))
    blk = lax.broadcasted_iota(jnp.int32, (n_sel, 1), 0)
    cur = qpos // NSA_SEL_BLOCK
    forced = (blk == 0) | (blk == cur) | (blk == cur - 1)
    score = jnp.where(forced, FORCE_SCORE, jnp.where(blk <= cur, imp, NEG_INF))
    sel_ref[...] = _rank_select(score, blk <= cur, NSA_SEL_TOPK)

    causal = _causal_t(t)
    per_tile = t // NSA_SEL_BLOCK

    def sweep(k_ref, vt_ref, kb, delta, mls, mask):
        k = k_ref[0, 0, _key_tile(kb, t), :]
        v_t = vt_ref[0, :, _key_tile(kb, t)]
        scores = [jnp.dot(k, qh[r], preferred_element_type=F32) + bias_ref[r, delta] for r in range(R)]
        return _flash_steps(mls, [acc_ref.at[r] for r in range(R)], scores, [mask] * R, [v_t] * R)

    def chosen_rows(kb):
        rows = [jnp.broadcast_to(sel_ref[pl.ds(kb * per_tile + c, 1), :], (NSA_SEL_BLOCK, t))
                for c in range(per_tile)]
        return jnp.concatenate(rows, axis=0) > 0.5

    init = tuple(_flash_init(t) for _ in range(R))

    acc_ref[...] = jnp.zeros(acc_ref.shape, F32)
    mls = lax.fori_loop(0, qb, lambda kb, mls: sweep(ks_ref, vst_ref, kb, qb - kb, mls, chosen_rows(kb)), init)
    mls = sweep(ks_ref, vst_ref, qb, 0, mls, chosen_rows(qb) & causal)
    for r in range(R):
        osel_ref[r] = _flash_out(mls[r], acc_ref.at[r])

    acc_ref[...] = jnp.zeros(acc_ref.shape, F32)
    key_j = lax.broadcasted_iota(jnp.int32, (t, t), 0)
    qry_i = lax.broadcasted_iota(jnp.int32, (t, t), 1)
    mls = init
    for back in range(NSA_WINDOW // t, -1, -1):
        dist = back * t + qry_i - key_j
        wmask = (dist >= 0) & (dist < NSA_WINDOW) & (qb >= back)
        mls = sweep(kw_ref, vwt_ref, jnp.maximum(qb - back, 0), back, mls, wmask)

    gates = jax.nn.sigmoid(g_ref[0, 0].astype(F32))
    for r in range(R):
        o = (gates[3 * r:3 * r + 1, :] * ocmp_ref[r]
             + gates[3 * r + 1:3 * r + 2, :] * osel_ref[r]
             + gates[3 * r + 2:3 * r + 3, :] * _flash_out(mls[r], acc_ref.at[r]))
        o_ref[0, r * d:(r + 1) * d, :] = o.astype(o_ref.dtype)


def _nsa_overlap_t(S):
    n_cmp_rows = S // NSA_CMP_STRIDE
    n_sel = S // NSA_SEL_BLOCK
    cmp_start = jnp.arange(n_cmp_rows) * NSA_CMP_STRIDE
    sel_start = jnp.arange(n_sel) * NSA_SEL_BLOCK
    overlap = jnp.clip(jnp.minimum(cmp_start[None, :] + NSA_CMP_BLOCK, sel_start[:, None] + NSA_SEL_BLOCK)
                       - jnp.maximum(cmp_start[None, :], sel_start[:, None]), 0).astype(F32) / NSA_CMP_STRIDE
    return jnp.where(cmp_start[None, :] + NSA_CMP_BLOCK <= S, overlap, 0.0).astype(BF16)


def _nsa_attention(q_t, kc, vc_t, ks, vs_t, kw, vw_t, gates_t, bias_tiles):
    B, _, S = q_t.shape
    t = ATT_TILE
    nq = S // t
    G, R, d = NSA_KV_GROUPS, NSA_GROUP_SIZE, HEAD_DIM
    n_cmp_rows = kc.shape[2]
    overlap_t = _nsa_overlap_t(S)
    n_sel = overlap_t.shape[0]
    k_spec = pl.BlockSpec((1, 1, S, d), lambda g, b, i: (b, g, 0, 0))
    vt_spec = pl.BlockSpec((1, d, S), lambda g, b, i: (b, g, 0))
    return pl.pallas_call(
        _nsa_kernel,
        grid=(G, B, nq),
        in_specs=[pl.BlockSpec((1, R * d, t), lambda g, b, i: (b, g, i)),
                  pl.BlockSpec((1, 1, n_cmp_rows, d), lambda g, b, i: (b, g, 0, 0)),
                  pl.BlockSpec((1, 1, d, n_cmp_rows), lambda g, b, i: (b, g, 0, 0)),
                  k_spec, vt_spec, k_spec, vt_spec,
                  pl.BlockSpec((1, 1, 3 * R, t), lambda g, b, i: (b, g, 0, i)),
                  pl.BlockSpec((R, nq, t, t), lambda g, b, i: (g, 0, 0, 0)),
                  pl.BlockSpec((n_sel, n_cmp_rows), lambda g, b, i: (0, 0))],
        out_specs=pl.BlockSpec((1, R * d, t), lambda g, b, i: (b, g, i)),
        out_shape=jax.ShapeDtypeStruct((B, G * R * d, S), BF16),
        scratch_shapes=[pltpu.VMEM((n_sel, t), F32),
                        pltpu.VMEM((R, d, t), F32),
                        pltpu.VMEM((R, d, t), F32),
                        pltpu.VMEM((R, d, t), F32)],
        compiler_params=_cparams(("parallel", "parallel", "arbitrary")),
        name="nsa_attention",
    )(q_t, kc, vc_t, ks, vs_t, kw, vw_t, gates_t, bias_tiles, overlap_t)


def _qkv_head_gain(q_gain, k_gain, n_heads, n_plain):
    return jnp.concatenate([jnp.tile(q_gain, n_heads) * Q_SCALE, jnp.tile(k_gain, n_heads),
                            jnp.ones((n_plain,), F32)])


def _swap_last(x):
    return x.transpose(0, 2, 1)


def _diff_layer(h2, B, S, norm_mix, p, bias_tiles, lambda_init):
    w_in, w_out, q_gain, k_gain, lq1, lk1, lq2, lk2, subln = p
    hg = _qkv_head_gain(q_gain, k_gain, 2 * DIFF_HEADS, D_MODEL)
    qkv = _norm_proj(h2, norm_mix, w_in.astype(BF16), hg, 2 * D_MODEL).reshape(B, S, -1)
    o_t = _diff_attention(_swap_last(qkv[:, :, :D_MODEL]), qkv, _swap_last(qkv[:, :, 2 * D_MODEL:]),
                          jnp.stack([lq1, lk1, lq2, lk2]), subln, bias_tiles, lambda_init)
    return _proj_residual(_swap_last(o_t).reshape(B * S, -1), w_out.astype(BF16), h2)


def _moba_layer(h2, B, S, norm_mix, p, bias_tiles):
    w_in, w_out, q_gain, k_gain = p
    hg = _qkv_head_gain(q_gain, k_gain, N_SCORE_HEADS, D_MODEL)
    qkv = _norm_proj(h2, norm_mix, w_in.astype(BF16), hg, 2 * D_MODEL).reshape(B, S, -1)
    o_t = _moba_attention(_swap_last(qkv[:, :, :D_MODEL]), qkv, _swap_last(qkv[:, :, 2 * D_MODEL:]), bias_tiles)
    return _proj_residual(_swap_last(o_t).reshape(B * S, -1), w_out.astype(BF16), h2)


def _nsa_layer(h2, B, S, norm_mix, p, bias_tiles):
    w_in, w_out, q_gain, k_gain, pos_k, pos_v, k_w1, k_w2, v_w1, v_w2 = p
    G, d = NSA_KV_GROUPS, HEAD_DIM
    kvw = G * d
    qw = N_SCORE_HEADS * d
    sec = {name: slice(qw + i * kvw, qw + (i + 1) * kvw) for i, name in
           enumerate(("kc", "vc", "ks", "vs", "kw", "vw"))}
    n_gate = 3 * N_SCORE_HEADS
    tn = 384
    n_used = qw + 6 * kvw + n_gate
    n_pad = -n_used % tn
    w_perm = jnp.concatenate([w_in[:, :qw], w_in[:, sec["ks"]], w_in[:, sec["kw"]], w_in[:, sec["kc"]],
                              w_in[:, sec["vc"]], w_in[:, sec["vs"]], w_in[:, sec["vw"]],
                              w_in[:, qw + 6 * kvw:], jnp.zeros((D_MODEL, n_pad), F32)], axis=1)
    hg = jnp.concatenate([jnp.tile(q_gain, N_SCORE_HEADS) * Q_SCALE, jnp.tile(k_gain[1], G),
                          jnp.tile(k_gain[2], G), jnp.ones((n_used + n_pad - qw - 2 * kvw,), F32)])
    proj = _norm_proj(h2, norm_mix, w_perm.astype(BF16), hg, qw + 2 * kvw, tn=tn).reshape(B, S, -1)

    def section(i):
        return proj[:, :, qw + i * kvw:qw + (i + 1) * kvw]

    def group_major(x):
        return x.reshape(B, S, G, d).transpose(0, 2, 1, 3)

    ks, kw, kc, vc, vs, vw = (section(i) for i in range(6))
    gates_t = _swap_last(proj[:, :, qw + 6 * kvw:n_used]).reshape(B, G, n_gate // G, S)
    k_cmp = _compress(group_major(kc), pos_k, k_w1.astype(BF16), k_w2.astype(BF16), k_gain[0], True)
    v_cmp = _compress(group_major(vc), pos_v, v_w1.astype(BF16), v_w2.astype(BF16), k_gain[0], False)
    o_t = _nsa_attention(_swap_last(proj[:, :, :qw]), k_cmp, v_cmp.transpose(0, 1, 3, 2),
                         group_major(ks), _swap_last(vs), group_major(kw), _swap_last(vw), gates_t, bias_tiles)
    return _proj_residual(_swap_last(o_t).reshape(B * S, qw), w_out.astype(BF16), h2)


def _diff_lambda_init(layer):
    return 0.8 - 0.6 * math.exp(-0.3 * layer)


def kernel(x, rel_bias,
           l0_norm_mix, l0_w_in, l0_w_out, l0_q_gain, l0_k_gain, l0_lam_q1, l0_lam_k1, l0_lam_q2, l0_lam_k2,
           l0_subln_gain, l0_norm_ffn, l0_w_gate, l0_w_up, l0_w_down,
           l1_norm_mix, l1_w_in, l1_w_out, l1_q_gain, l1_k_gain, l1_norm_ffn, l1_w_router, l1_b_router,
           l1_we_gate, l1_we_up, l1_we_down,
           l2_norm_mix, l2_w_in, l2_w_out, l2_q_gain, l2_k_gain, l2_cmp_pos_k, l2_cmp_pos_v, l2_cmp_k_w1,
           l2_cmp_k_w2, l2_cmp_v_w1, l2_cmp_v_w2, l2_norm_ffn, l2_w_gate, l2_w_up, l2_w_down,
           l3_norm_mix, l3_w_in, l3_w_out, l3_q_gain, l3_k_gain, l3_lam_q1, l3_lam_k1, l3_lam_q2, l3_lam_k2,
           l3_subln_gain, l3_norm_ffn, l3_w_router, l3_b_router, l3_we_gate, l3_we_up, l3_we_down):
    B, S, D = x.shape
    bias_tiles = _bias_tiles(rel_bias, S, ATT_TILE)
    h = x.reshape(B * S, D)

    h = _diff_layer(h, B, S, l0_norm_mix, (l0_w_in, l0_w_out, l0_q_gain, l0_k_gain, l0_lam_q1, l0_lam_k1,
                                           l0_lam_q2, l0_lam_k2, l0_subln_gain), bias_tiles, _diff_lambda_init(0))
    h = _dense_ffn(h, l0_norm_ffn, l0_w_gate.astype(BF16), l0_w_up.astype(BF16), l0_w_down.astype(BF16))

    h = _moba_layer(h, B, S, l1_norm_mix, (l1_w_in, l1_w_out, l1_q_gain, l1_k_gain), bias_tiles)
    h = _moe_ffn(h, l1_norm_ffn, l1_w_router, l1_b_router, l1_we_gate.astype(BF16), l1_we_up.astype(BF16),
                 l1_we_down.astype(BF16))

    h = _nsa_layer(h, B, S, l2_norm_mix, (l2_w_in, l2_w_out, l2_q_gain, l2_k_gain, l2_cmp_pos_k, l2_cmp_pos_v,
                                          l2_cmp_k_w1, l2_cmp_k_w2, l2_cmp_v_w1, l2_cmp_v_w2), bias_tiles)
    h = _dense_ffn(h, l2_norm_ffn, l2_w_gate.astype(BF16), l2_w_up.astype(BF16), l2_w_down.astype(BF16))

    h = _diff_layer(h, B, S, l3_norm_mix, (l3_w_in, l3_w_out, l3_q_gain, l3_k_gain, l3_lam_q1, l3_lam_k1,
                                           l3_lam_q2, l3_lam_k2, l3_subln_gain), bias_tiles, _diff_lambda_init(3))
    h = _moe_ffn(h, l3_norm_ffn, l3_w_router, l3_b_router, l3_we_gate.astype(BF16), l3_we_up.astype(BF16),
                 l3_we_down.astype(BF16))
    return h.reshape(B, S, D)
```

```python
import functools
import math

import jax
import jax.numpy as jnp
from jax import lax
from jax.experimental import pallas as pl
from jax.experimental.pallas import tpu as pltpu

F32 = jnp.float32
BF16 = jnp.bfloat16

D_MODEL = 1024
HEAD_DIM = 64
N_SCORE_HEADS = 16
DIFF_HEADS = 8
MOBA_BLOCK = 256
MOBA_TOPK = 3
NSA_KV_GROUPS = 4
NSA_GROUP_SIZE = 4
NSA_CMP_BLOCK = 32
NSA_CMP_STRIDE = 16
NSA_CMP_HIDDEN = 256
NSA_SEL_BLOCK = 64
NSA_SEL_TOPK = 16
NSA_WINDOW = 512
REL_BUCKETS = 32
REL_MAX_DIST = 1024
FFN_DIM = 3584
N_EXPERTS = 8
NORM_EPS = 1e-6
NEG_INF = -1e30
FORCE_SCORE = 1e30
ATTN_SCALE = HEAD_DIM ** -0.5
LOG2E = math.log2(math.e)
Q_SCALE = ATTN_SCALE * LOG2E

LANES = 128
ATT_TILE = 256
ATT_BATCH = 2
VMEM_LIMIT = 56 * 1024 * 1024


def _cparams(sem):
    return pltpu.CompilerParams(dimension_semantics=sem, vmem_limit_bytes=VMEM_LIMIT)


def _dot_nt(a, b):
    return lax.dot_general(a, b, (((1,), (1,)), ((), ())), preferred_element_type=F32)


def _norm_proj_kernel(x_ref, g_ref, w_ref, hg_ref, gm_ref, o_ref, xn_ref, *, n_norm_tiles):
    j = pl.program_id(1)

    @pl.when(j == 0)
    def _():
        x = x_ref[...]
        r = lax.rsqrt(jnp.mean(x * x, axis=-1, keepdims=True) + NORM_EPS)
        xn_ref[...] = (x * r * g_ref[...]).astype(BF16)

    acc = jnp.dot(xn_ref[...], w_ref[...], preferred_element_type=F32)

    @pl.when(j < n_norm_tiles)
    def _():
        sq = acc * acc
        hi = sq.astype(BF16)
        lo = (sq - hi.astype(F32)).astype(BF16)
        ms = (jnp.dot(hi, gm_ref[...], preferred_element_type=F32)
              + jnp.dot(lo, gm_ref[...], preferred_element_type=F32))
        o_ref[...] = (acc * lax.rsqrt(ms + NORM_EPS) * hg_ref[...]).astype(o_ref.dtype)

    @pl.when(j >= n_norm_tiles)
    def _():
        o_ref[...] = acc.astype(o_ref.dtype)


def _norm_proj(h2, norm_gain, w, head_gain, n_norm_cols, *, tm=1024, tn=512):
    T, D = h2.shape
    N = w.shape[1]
    assert T % tm == 0 and N % tn == 0 and n_norm_cols % tn == 0 and tn % HEAD_DIM == 0
    col = jnp.arange(tn) // HEAD_DIM
    group_mean = jnp.where(col[:, None] == col[None, :], 1.0 / HEAD_DIM, 0.0).astype(BF16)
    return pl.pallas_call(
        functools.partial(_norm_proj_kernel, n_norm_tiles=n_norm_cols // tn),
        grid=(T // tm, N // tn),
        in_specs=[pl.BlockSpec((tm, D), lambda i, j: (i, 0)),
                  pl.BlockSpec((1, D), lambda i, j: (0, 0)),
                  pl.BlockSpec((D, tn), lambda i, j: (0, j)),
                  pl.BlockSpec((1, tn), lambda i, j: (0, j)),
                  pl.BlockSpec((tn, tn), lambda i, j: (0, 0))],
        out_specs=pl.BlockSpec((tm, tn), lambda i, j: (i, j)),
        out_shape=jax.ShapeDtypeStruct((T, N), BF16),
        scratch_shapes=[pltpu.VMEM((tm, D), BF16)],
        compiler_params=_cparams(("parallel", "arbitrary")),
        name="norm_proj",
    )(h2, norm_gain.reshape(1, D), w, head_gain.reshape(1, N), group_mean)


def _proj_res_kernel(a_ref, w_ref, r_ref, o_ref):
    o_ref[...] = r_ref[...] + jnp.dot(a_ref[...], w_ref[...], preferred_element_type=F32)


def _proj_residual(a, w, res, *, tm=512, tn=512):
    T, K = a.shape
    N = w.shape[1]
    return pl.pallas_call(
        _proj_res_kernel,
        grid=(T // tm, N // tn),
        in_specs=[pl.BlockSpec((tm, K), lambda i, j: (i, 0)),
                  pl.BlockSpec((K, tn), lambda i, j: (0, j)),
                  pl.BlockSpec((tm, tn), lambda i, j: (i, j))],
        out_specs=pl.BlockSpec((tm, tn), lambda i, j: (i, j)),
        out_shape=jax.ShapeDtypeStruct((T, N), F32),
        compiler_params=_cparams(("parallel", "arbitrary")),
        name="proj_residual",
    )(a, w, res)


def _ffn_kernel(h_ref, g_ref, wg_ref, wu_ref, wd_ref, o_ref, hn_ref):
    f = pl.program_id(1)

    @pl.when(f == 0)
    def _():
        x = h_ref[...]
        r = lax.rsqrt(jnp.mean(x * x, axis=-1, keepdims=True) + NORM_EPS)
        hn_ref[...] = (x * r * g_ref[...]).astype(BF16)
        o_ref[...] = x

    hn = hn_ref[...]
    gate = jnp.dot(hn, wg_ref[...], preferred_element_type=F32)
    up = jnp.dot(hn, wu_ref[...], preferred_element_type=F32)
    act = (gate * jax.nn.sigmoid(gate) * up).astype(BF16)
    o_ref[...] += jnp.dot(act, wd_ref[...], preferred_element_type=F32)


def _dense_ffn(h2, norm_gain, wg, wu, wd, *, tm=1024, tf=512):
    T, D = h2.shape
    F = wg.shape[1]
    return pl.pallas_call(
        _ffn_kernel,
        grid=(T // tm, F // tf),
        in_specs=[pl.BlockSpec((tm, D), lambda i, f: (i, 0)),
                  pl.BlockSpec((1, D), lambda i, f: (0, 0)),
                  pl.BlockSpec((D, tf), lambda i, f: (0, f)),
                  pl.BlockSpec((D, tf), lambda i, f: (0, f)),
                  pl.BlockSpec((tf, D), lambda i, f: (f, 0))],
        out_specs=pl.BlockSpec((tm, D), lambda i, f: (i, 0)),
        out_shape=jax.ShapeDtypeStruct((T, D), F32),
        scratch_shapes=[pltpu.VMEM((tm, D), BF16)],
        compiler_params=_cparams(("parallel", "arbitrary")),
        name="dense_ffn",
    )(h2, norm_gain.reshape(1, D), wg, wu, wd)


def _router_kernel(h_ref, g_ref, wr_ref, br_ref, gate_ref):
    x = h_ref[...]
    r = lax.rsqrt(jnp.mean(x * x, axis=-1, keepdims=True) + NORM_EPS)
    hn = x * r * g_ref[...]
    logits = jnp.dot(hn, wr_ref[...], preferred_element_type=F32,
                     precision=lax.Precision.HIGHEST) + br_ref[...]
    lane = lax.broadcasted_iota(jnp.int32, logits.shape, 1)
    logits = jnp.where(lane < N_EXPERTS, logits, NEG_INF)
    m1 = jnp.max(logits, axis=-1, keepdims=True)
    i1 = jnp.min(jnp.where(logits == m1, lane, LANES), axis=-1, keepdims=True)
    rest = jnp.where(lane == i1, NEG_INF, logits)
    m2 = jnp.max(rest, axis=-1, keepdims=True)
    i2 = jnp.min(jnp.where(rest == m2, lane, LANES), axis=-1, keepdims=True)
    e2 = jnp.exp(m2 - m1)
    w1 = 1.0 / (1.0 + e2)
    w2 = e2 / (1.0 + e2)
    gate_ref[...] = (jnp.where(lane == 0, i1.astype(F32), 0.0) + jnp.where(lane == 1, i2.astype(F32), 0.0)
                     + jnp.where(lane == 2, w1, 0.0) + jnp.where(lane == 3, w2, 0.0))


def _router(h2, norm_gain, w_router, b_router, *, tm=512):
    T, D = h2.shape
    wr = jnp.zeros((D, LANES), F32).at[:, :N_EXPERTS].set(w_router)
    br = jnp.zeros((1, LANES), F32).at[0, :N_EXPERTS].set(b_router)
    return pl.pallas_call(
        _router_kernel,
        grid=(T // tm,),
        in_specs=[pl.BlockSpec((tm, D), lambda i: (i, 0)),
                  pl.BlockSpec((1, D), lambda i: (0, 0)),
                  pl.BlockSpec((D, LANES), lambda i: (0, 0)),
                  pl.BlockSpec((1, LANES), lambda i: (0, 0))],
        out_specs=pl.BlockSpec((tm, LANES), lambda i: (i, 0)),
        out_shape=jax.ShapeDtypeStruct((T, LANES), F32),
        compiler_params=_cparams(("parallel",)),
        name="moe_router",
    )(h2, norm_gain.reshape(1, D), wr, br)


MOE_TILE = 512


def _gather_rows_kernel(idx_hbm, src_hbm, o_ref, idx_smem, idx_sem, row_sem):
    i = pl.program_id(0)
    n = o_ref.shape[0]
    idx_copy = pltpu.make_async_copy(idx_hbm.at[i], idx_smem, idx_sem)
    idx_copy.start()
    idx_copy.wait()

    def row_copy(r, src_row):
        return pltpu.make_async_copy(src_hbm.at[pl.ds(src_row, 1), :], o_ref.at[pl.ds(r, 1), :], row_sem)

    def issue(r, carry):
        row_copy(r, idx_smem[r]).start()
        return carry

    def drain(r, carry):
        row_copy(r, 0).wait()
        return carry

    lax.fori_loop(0, n, issue, 0, unroll=8)
    lax.fori_loop(0, n, drain, 0, unroll=8)


def _gather_rows(src, idx):
    M = idx.shape[0]
    D = src.shape[1]
    n = MOE_TILE
    assert M % n == 0
    return pl.pallas_call(
        _gather_rows_kernel,
        grid=(M // n,),
        in_specs=[pl.BlockSpec(memory_space=pl.ANY), pl.BlockSpec(memory_space=pl.ANY)],
        out_specs=pl.BlockSpec((n, D), lambda i: (i, 0)),
        out_shape=jax.ShapeDtypeStruct((M, D), src.dtype),
        scratch_shapes=[pltpu.SMEM((n,), jnp.int32), pltpu.SemaphoreType.DMA, pltpu.SemaphoreType.DMA],
        compiler_params=_cparams(("arbitrary",)),
        name="gather_rows",
    )(idx.reshape(M // n, n), src)


def _moe_expert_kernel(te_ref, nu_ref, x_ref, g_ref, wg_ref, wu_ref, wd_ref, o_ref, hn_ref):
    i = pl.program_id(0)
    f = pl.program_id(1)

    @pl.when(f == 0)
    def _():
        x = x_ref[...]
        r = lax.rsqrt(jnp.mean(x * x, axis=-1, keepdims=True) + NORM_EPS)
        hn_ref[...] = (x * r * g_ref[...]).astype(BF16)
        o_ref[...] = jnp.zeros(o_ref.shape, F32)

    @pl.when(i < nu_ref[0])
    def _():
        hn = hn_ref[...]
        gate = jnp.dot(hn, wg_ref[0], preferred_element_type=F32)
        up = jnp.dot(hn, wu_ref[0], preferred_element_type=F32)
        act = (gate * jax.nn.sigmoid(gate) * up).astype(BF16)
        o_ref[...] += jnp.dot(act, wd_ref[0], preferred_element_type=F32)


def _moe_experts(xs, norm_gain, tile_expert, n_used, we_gate, we_up, we_down, *, tf=512):
    N, D = xs.shape
    F = we_gate.shape[2]
    tm = MOE_TILE
    grid_spec = pltpu.PrefetchScalarGridSpec(
        num_scalar_prefetch=2,
        grid=(N // tm, F // tf),
        in_specs=[pl.BlockSpec((tm, D), lambda i, f, te, nu: (i, 0)),
                  pl.BlockSpec((1, D), lambda i, f, te, nu: (0, 0)),
                  pl.BlockSpec((1, D, tf), lambda i, f, te, nu: (te[i], 0, f)),
                  pl.BlockSpec((1, D, tf), lambda i, f, te, nu: (te[i], 0, f)),
                  pl.BlockSpec((1, tf, D), lambda i, f, te, nu: (te[i], f, 0))],
        out_specs=pl.BlockSpec((tm, D), lambda i, f, te, nu: (i, 0)),
        scratch_shapes=[pltpu.VMEM((tm, D), BF16)])
    return pl.pallas_call(
        _moe_expert_kernel,
        grid_spec=grid_spec,
        out_shape=jax.ShapeDtypeStruct((N, D), F32),
        compiler_params=_cparams(("arbitrary", "arbitrary")),
        name="moe_experts",
    )(tile_expert, n_used, xs, norm_gain.reshape(1, D), we_gate, we_up, we_down)


def _moe_combine_kernel(h_ref, r_ref, y0_ref, y1_ref, o_ref):
    r = r_ref[...]
    o_ref[...] = h_ref[...] + r[:, 2:3] * y0_ref[...] + r[:, 3:4] * y1_ref[...]


def _moe_combine(h2, route, yg, *, tm=512):
    T, D = h2.shape
    nt = T // tm
    return pl.pallas_call(
        _moe_combine_kernel,
        grid=(nt,),
        in_specs=[pl.BlockSpec((tm, D), lambda i: (i, 0)),
                  pl.BlockSpec((tm, LANES), lambda i: (i, 0)),
                  pl.BlockSpec((tm, D), lambda i: (i, 0)),
                  pl.BlockSpec((tm, D), lambda i: (i + nt, 0))],
        out_specs=pl.BlockSpec((tm, D), lambda i: (i, 0)),
        out_shape=jax.ShapeDtypeStruct((T, D), F32),
        compiler_params=_cparams(("parallel",)),
        name="moe_combine",
    )(h2, route, yg, yg)


def _moe_ffn(h2, norm_gain, w_router, b_router, we_gate, we_up, we_down):
    T, D = h2.shape
    E = we_gate.shape[0]
    tm = MOE_TILE
    route = _router(h2, norm_gain, w_router, b_router)
    expert = route[:, 0:2].astype(jnp.int32).T.reshape(-1)
    onehot = (expert[:, None] == jnp.arange(E)[None, :]).astype(jnp.int32)
    rank = jnp.sum((jnp.cumsum(onehot, axis=0) - onehot) * onehot, axis=1)
    tiles = (jnp.sum(onehot, axis=0) + tm - 1) // tm
    tile_end = jnp.cumsum(tiles)
    pos = ((tile_end - tiles) * tm)[expert] + rank
    n_rows = 2 * T + E * tm
    row_token = jnp.zeros((n_rows,), jnp.int32).at[pos].set(jnp.tile(jnp.arange(T, dtype=jnp.int32), 2))
    tile_expert = jnp.minimum(jnp.sum(jnp.arange(n_rows // tm)[:, None] >= tile_end[None, :], axis=1),
                              E - 1).astype(jnp.int32)
    xs = _gather_rows(h2, row_token)
    ys = _moe_experts(xs, norm_gain, tile_expert, tile_end[E - 1:].astype(jnp.int32), we_gate, we_up, we_down)
    return _moe_combine(h2, route, _gather_rows(ys, pos.astype(jnp.int32)))


def _rel_bucket(dist):
    n = jnp.maximum(dist, 0)
    max_exact = REL_BUCKETS // 2
    nf = jnp.maximum(n, max_exact).astype(F32)
    large = max_exact + (jnp.log(nf / max_exact) / math.log(REL_MAX_DIST / max_exact)
                         * (REL_BUCKETS - max_exact)).astype(jnp.int32)
    return jnp.where(n < max_exact, n, jnp.minimum(large, REL_BUCKETS - 1))


def _bias_tiles_kernel(tab_ref, idx_ref, o_ref):
    t = idx_ref.shape[-1]

    def chunk(c, carry):
        rows = pl.ds(pl.multiple_of(c * 8, 8), 8)
        idx = idx_ref[0, rows, :]
        for h in range(N_SCORE_HEADS):
            tile = jnp.zeros((8, t), F32)
            for b in range(REL_BUCKETS):
                tile = jnp.where(idx == b, tab_ref[h, b] * LOG2E, tile)
            o_ref[h, 0, rows, :] = tile
        return carry

    lax.fori_loop(0, t // 8, chunk, 0)


def _bias_tiles(rel_bias, S, t):
    n_delta = S // t
    i = jnp.arange(t)
    dist = (jnp.arange(n_delta) * t)[:, None, None] + i[None, None, :] - i[None, :, None]
    idx = _rel_bucket(dist).astype(jnp.int32)
    return pl.pallas_call(
        _bias_tiles_kernel,
        grid=(n_delta,),
        in_specs=[pl.BlockSpec(memory_space=pltpu.SMEM),
                  pl.BlockSpec((1, t, t), lambda d: (d, 0, 0))],
        out_specs=pl.BlockSpec((N_SCORE_HEADS, 1, t, t), lambda d: (0, d, 0, 0)),
        out_shape=jax.ShapeDtypeStruct((N_SCORE_HEADS, n_delta, t, t), F32),
        compiler_params=_cparams(("parallel",)),
        name="bias_tiles",
    )(rel_bias.T, idx)


def _flash_steps(mls, acc_refs, scores, masks, v_ts):
    new_mls, alphas, probs = [], [], []
    for (m, l), s, mask in zip(mls, scores, masks):
        if mask is not None:
            s = jnp.where(mask, s, NEG_INF)
        m_new = jnp.maximum(m, jnp.max(s, axis=0, keepdims=True))
        alpha = jnp.exp2(m - m_new)
        p = jnp.exp2(s - m_new)
        if mask is not None:
            p = jnp.where(mask, p, 0.0)
        new_mls.append((m_new, alpha * l + jnp.sum(p, axis=0, keepdims=True)))
        alphas.append(alpha)
        probs.append(p.astype(BF16))
    for acc_ref, alpha, p, v_t in zip(acc_refs, alphas, probs, v_ts):
        acc_ref[...] = alpha * acc_ref[...] + jnp.dot(v_t, p, preferred_element_type=F32)
    return tuple(new_mls)


def _flash_init(nq):
    return (jnp.full((1, nq), NEG_INF, F32), jnp.zeros((1, nq), F32))


def _flash_out(ml, acc_ref):
    return acc_ref[...] / jnp.maximum(ml[1], 1e-30)


def _key_tile(kb, t):
    return pl.ds(pl.multiple_of(kb * t, t), t)


def _causal_t(t):
    return lax.broadcasted_iota(jnp.int32, (t, t), 1) >= lax.broadcasted_iota(jnp.int32, (t, t), 0)


def _row_band(x, lo, hi):
    row = lax.broadcasted_iota(jnp.int32, (x.shape[0], 1), 0)
    return jnp.where((row >= lo) & (row < hi), x, jnp.zeros_like(x))


def _rank_select(score, cand_valid, topk):
    C = score.shape[0]
    row = lax.broadcasted_iota(jnp.int32, score.shape, 0)
    rank = jnp.zeros(score.shape, F32)
    for c in range(C):
        sc = score[c:c + 1, :]
        tie = (row > c).astype(F32)
        rank = rank + jnp.where(sc > score, 1.0, jnp.where(sc == score, tie, 0.0))
    return jnp.where(cand_valid & (rank < topk), 1.0, 0.0)


def _diff_attn_kernel(lam_ref, sub_ref, qt_ref, k_ref, vt_ref, bias_ref, o_ref, acc_ref, *, lambda_init):
    t = ATT_TILE
    nb = qt_ref.shape[0]
    qb = pl.program_id(2)
    chains = [(b, mi) for b in range(nb) for mi in range(2)]
    qm = [_row_band(qt_ref[b], mi * HEAD_DIM, (mi + 1) * HEAD_DIM) for b, mi in chains]
    acc_refs = [acc_ref.at[c] for c in range(len(chains))]
    acc_ref[...] = jnp.zeros(acc_ref.shape, F32)

    def step(kb, mls, mask):
        ks = [k_ref[b, _key_tile(kb, t), :] for b in range(nb)]
        v_ts = [vt_ref[b, :, _key_tile(kb, t)] for b in range(nb)]
        scores = [jnp.dot(ks[b], qm[c], preferred_element_type=F32) + bias_ref[mi, qb - kb]
                  for c, (b, mi) in enumerate(chains)]
        return _flash_steps(mls, acc_refs, scores, [mask] * len(chains), [v_ts[b] for b, _ in chains])

    mls = lax.fori_loop(0, qb, lambda kb, mls: step(kb, mls, None), tuple(_flash_init(t) for _ in chains))
    mls = step(qb, mls, _causal_t(t))
    lv = lam_ref[...]
    lam = (jnp.exp(jnp.sum(lv[0:1] * lv[1:2], axis=-1, keepdims=True))
           - jnp.exp(jnp.sum(lv[2:3] * lv[3:4], axis=-1, keepdims=True)) + lambda_init)
    for b in range(nb):
        o = _flash_out(mls[2 * b], acc_refs[2 * b]) - lam * _flash_out(mls[2 * b + 1], acc_refs[2 * b + 1])
        r = lax.rsqrt(jnp.mean(o * o, axis=0, keepdims=True) + NORM_EPS)
        o_ref[b] = (o * r * sub_ref[...] * (1.0 - lambda_init)).astype(o_ref.dtype)


def _diff_attention(q_t, qkv, v_t, lam4, subln_gain, bias_tiles, lambda_init):
    B, S, _ = qkv.shape
    t = ATT_TILE
    nq = S // t
    H = DIFF_HEADS
    nb = ATT_BATCH
    assert B % nb == 0
    return pl.pallas_call(
        functools.partial(_diff_attn_kernel, lambda_init=lambda_init),
        grid=(H, B // nb, nq),
        in_specs=[pl.BlockSpec((4, HEAD_DIM), lambda h, b, i: (0, 0)),
                  pl.BlockSpec((LANES, 1), lambda h, b, i: (0, 0)),
                  pl.BlockSpec((nb, LANES, t), lambda h, b, i: (b, h, i)),
                  pl.BlockSpec((nb, S, LANES), lambda h, b, i: (b, 0, H + h)),
                  pl.BlockSpec((nb, LANES, S), lambda h, b, i: (b, h, 0)),
                  pl.BlockSpec((2, nq, t, t), lambda h, b, i: (h, 0, 0, 0))],
        out_specs=pl.BlockSpec((nb, LANES, t), lambda h, b, i: (b, h, i)),
        out_shape=jax.ShapeDtypeStruct((B, H * LANES, S), BF16),
        scratch_shapes=[pltpu.VMEM((2 * nb, LANES, t), F32)],
        compiler_params=_cparams(("parallel", "parallel", "arbitrary")),
        name="diff_attention",
    )(lam4, subln_gain.reshape(LANES, 1), q_t, qkv, v_t, bias_tiles)


def _moba_kernel(qt_ref, k_ref, vt_ref, bias_ref, o_ref, kmean_ref, sel_ref, acc_ref):
    t = ATT_TILE
    nb = qt_ref.shape[0]
    qb = pl.program_id(2)
    n_blk = kmean_ref.shape[1]

    @pl.when(qb == 0)
    def _():
        for b in range(nb):
            for n in range(n_blk):
                kb = k_ref[b, n * t:(n + 1) * t, :].astype(F32)
                kmean_ref[b, n:n + 1, :] = jnp.mean(kb, axis=0, keepdims=True)

    chains = [(b, hj) for b in range(nb) for hj in range(2)]
    qm = [_row_band(qt_ref[b], hj * HEAD_DIM, (hj + 1) * HEAD_DIM) for b, hj in chains]
    acc_refs = [acc_ref.at[c] for c in range(len(chains))]
    past = lax.broadcasted_iota(jnp.int32, (n_blk, t), 0) < qb
    for c, (b, hj) in enumerate(chains):
        km = kmean_ref[b]
        km_hi = km.astype(BF16)
        km_lo = (km - km_hi.astype(F32)).astype(BF16)
        gate = (jnp.dot(km_hi, qm[c], preferred_element_type=F32)
                + jnp.dot(km_lo, qm[c], preferred_element_type=F32))
        sel_ref[c] = _rank_select(jnp.where(past, gate, NEG_INF), past, MOBA_TOPK)
    acc_ref[...] = jnp.zeros(acc_ref.shape, F32)

    def step(kb, mls, diagonal):
        ks = [k_ref[b, _key_tile(kb, t), :] for b in range(nb)]
        v_ts = [vt_ref[b, :, _key_tile(kb, t)] for b in range(nb)]
        scores = [jnp.dot(ks[b], qm[c], preferred_element_type=F32) + bias_ref[hj, qb - kb]
                  for c, (b, hj) in enumerate(chains)]
        masks = [_causal_t(t) if diagonal else sel_ref[c, pl.ds(kb, 1), :] > 0.5 for c in range(len(chains))]
        return _flash_steps(mls, acc_refs, scores, masks,
                            [v_ts[b][hj * HEAD_DIM:(hj + 1) * HEAD_DIM, :] for b, hj in chains])

    mls = lax.fori_loop(0, qb, lambda kb, mls: step(kb, mls, False), tuple(_flash_init(t) for _ in chains))
    mls = step(qb, mls, True)
    for c, (b, hj) in enumerate(chains):
        o_ref[b, hj * HEAD_DIM:(hj + 1) * HEAD_DIM, :] = _flash_out(mls[c], acc_refs[c]).astype(o_ref.dtype)


def _moba_attention(q_t, qkv, v_t, bias_tiles):
    B, S, _ = qkv.shape
    t = ATT_TILE
    assert t == MOBA_BLOCK and S % t == 0
    nq = S // t
    HP = D_MODEL // LANES
    nb = ATT_BATCH
    assert B % nb == 0
    return pl.pallas_call(
        _moba_kernel,
        grid=(HP, B // nb, nq),
        in_specs=[pl.BlockSpec((nb, LANES, t), lambda h, b, i: (b, h, i)),
                  pl.BlockSpec((nb, S, LANES), lambda h, b, i: (b, 0, HP + h)),
                  pl.BlockSpec((nb, LANES, S), lambda h, b, i: (b, h, 0)),
                  pl.BlockSpec((2, nq, t, t), lambda h, b, i: (h, 0, 0, 0))],
        out_specs=pl.BlockSpec((nb, LANES, t), lambda h, b, i: (b, h, i)),
        out_shape=jax.ShapeDtypeStruct((B, D_MODEL, S), BF16),
        scratch_shapes=[pltpu.VMEM((nb, S // t, LANES), F32),
                        pltpu.VMEM((2 * nb, S // t, t), F32),
                        pltpu.VMEM((2 * nb, HEAD_DIM, t), F32)],
        compiler_params=_cparams(("parallel", "parallel", "arbitrary")),
        name="moba_attention",
    )(q_t, qkv, v_t, bias_tiles)


def _compress_kernel(x_ref, pos_ref, w1_ref, w2_ref, gain_ref, o_ref, *, normalize):
    G, n_half, width = x_ref.shape[1:]
    x = x_ref[0].reshape(G * n_half, width).astype(F32)
    first = (x + pos_ref[0:1, :]).astype(BF16)
    second = (x + pos_ref[1:2, :]).astype(BF16)
    u = jnp.dot(first, w1_ref[0:width, :], preferred_element_type=F32)
    low = jnp.dot(second, w1_ref[width:2 * width, :], preferred_element_type=F32)
    rows = G * n_half
    hid = u + pltpu.roll(low, rows - 1, 0)
    c = jnp.dot(jax.nn.gelu(hid).astype(BF16), w2_ref[...], preferred_element_type=F32)
    if normalize:
        r = lax.rsqrt(jnp.mean(c * c, axis=-1, keepdims=True) + NORM_EPS)
        c = c * r * gain_ref[...]
    o_ref[0] = c.reshape(G, n_half, HEAD_DIM).astype(o_ref.dtype)


def _compress(x, pos, w1, w2, gain, normalize):
    B, G, S, d = x.shape
    n_half = S // NSA_CMP_STRIDE
    width = NSA_CMP_STRIDE * d
    return pl.pallas_call(
        functools.partial(_compress_kernel, normalize=normalize),
        grid=(B,),
        in_specs=[pl.BlockSpec((1, G, n_half, width), lambda b: (b, 0, 0, 0)),
                  pl.BlockSpec((2, width), lambda b: (0, 0)),
                  pl.BlockSpec((2 * width, NSA_CMP_HIDDEN), lambda b: (0, 0)),
                  pl.BlockSpec((NSA_CMP_HIDDEN, d), lambda b: (0, 0)),
                  pl.BlockSpec((1, d), lambda b: (0, 0))],
        out_specs=pl.BlockSpec((1, G, n_half, d), lambda b: (b, 0, 0, 0)),
        out_shape=jax.ShapeDtypeStruct((B, G, n_half, d), BF16),
        compiler_params=_cparams(("parallel",)),
        name="nsa_compress",
    )(x.reshape(B, G, n_half, width), pos.reshape(2, width), w1, w2, gain.reshape(1, d))


def _nsa_kernel(qt_ref, kc_ref, vct_ref, ks_ref, vst_ref, kw_ref, vwt_ref, g_ref, bias_ref,
                ov_ref, o_ref, sel_ref, acc_ref, ocmp_ref, osel_ref):
    t = ATT_TILE
    R = NSA_GROUP_SIZE
    d = HEAD_DIM
    qb = pl.program_id(2)
    n_cmp_rows = kc_ref.shape[2]
    n_sel = ov_ref.shape[0]
    qt = qt_ref[0]
    qh = [qt[r * d:(r + 1) * d, :] for r in range(R)]
    qpos = qb * t + lax.broadcasted_iota(jnp.int32, (1, t), 1)

    n_idx = lax.broadcasted_iota(jnp.int32, (n_cmp_rows, 1), 0)
    cmask = (n_idx * NSA_CMP_STRIDE + (NSA_CMP_BLOCK - 1) <= qpos) & (n_idx < n_cmp_rows - 1)
    kc = kc_ref[0, 0]
    psum = jnp.zeros((n_cmp_rows, t), F32)
    for r in range(R):
        cl = jnp.where(cmask, jnp.dot(kc, qh[r], preferred_element_type=F32), NEG_INF)
        cm = jnp.max(cl, axis=0, keepdims=True)
        cp = jnp.where(cmask, jnp.exp2(cl - cm), 0.0)
        cp = cp / jnp.maximum(jnp.sum(cp, axis=0, keepdims=True), 1e-30)
        ocmp_ref[r] = jnp.dot(vct_ref[0, 0], cp.astype(BF16), preferred_element_type=F32)
        psum = psum + cp

    p_hi = psum.astype(BF16)
    p_lo = (psum - p_hi.astype(F32)).astype(BF16)
    imp = (jnp.dot(ov_ref[...], p_hi, preferred_element_type=F32)
           + jnp.dot(ov_ref[...], p_lo, preferred_element_type=F32))
    blk = lax.broadcasted_iota(jnp.int32, (n_sel, 1), 0)
    cur = qpos // NSA_SEL_BLOCK
    forced = (blk == 0) | (blk == cur) | (blk == cur - 1)
    score = jnp.where(forced, FORCE_SCORE, jnp.where(blk <= cur, imp, NEG_INF))
    sel_ref[...] = _rank_select(score, blk <= cur, NSA_SEL_TOPK)

    causal = _causal_t(t)
    per_tile = t // NSA_SEL_BLOCK

    def sweep(k_ref, vt_ref, kb, delta, mls, mask):
        k = k_ref[0, 0, _key_tile(kb, t), :]
        v_t = vt_ref[0, :, _key_tile(kb, t)]
        scores = [jnp.dot(k, qh[r], preferred_element_type=F32) + bias_ref[r, delta] for r in range(R)]
        return _flash_steps(mls, [acc_ref.at[r] for r in range(R)], scores, [mask] * R, [v_t] * R)

    def chosen_rows(kb):
        rows = [jnp.broadcast_to(sel_ref[pl.ds(kb * per_tile + c, 1), :], (NSA_SEL_BLOCK, t))
                for c in range(per_tile)]
        return jnp.concatenate(rows, axis=0) > 0.5

    init = tuple(_flash_init(t) for _ in range(R))

    acc_ref[...] = jnp.zeros(acc_ref.shape, F32)
    mls = lax.fori_loop(0, qb, lambda kb, mls: sweep(ks_ref, vst_ref, kb, qb - kb, mls, chosen_rows(kb)), init)
    mls = sweep(ks_ref, vst_ref, qb, 0, mls, chosen_rows(qb) & causal)
    for r in range(R):
        osel_ref[r] = _flash_out(mls[r], acc_ref.at[r])

    acc_ref[...] = jnp.zeros(acc_ref.shape, F32)
    key_j = lax.broadcasted_iota(jnp.int32, (t, t), 0)
    qry_i = lax.broadcasted_iota(jnp.int32, (t, t), 1)
    mls = init
    for back in range(NSA_WINDOW // t, -1, -1):
        dist = back * t + qry_i - key_j
        wmask = (dist >= 0) & (dist < NSA_WINDOW) & (qb >= back)
        mls = sweep(kw_ref, vwt_ref, jnp.maximum(qb - back, 0), back, mls, wmask)

    gates = jax.nn.sigmoid(g_ref[0, 0].astype(F32))
    for r in range(R):
        o = (gates[3 * r:3 * r + 1, :] * ocmp_ref[r]
             + gates[3 * r + 1:3 * r + 2, :] * osel_ref[r]
             + gates[3 * r + 2:3 * r + 3, :] * _flash_out(mls[r], acc_ref.at[r]))
        o_ref[0, r * d:(r + 1) * d, :] = o.astype(o_ref.dtype)


def _nsa_overlap_t(S):
    n_cmp_rows = S // NSA_CMP_STRIDE
    n_sel = S // NSA_SEL_BLOCK
    cmp_start = jnp.arange(n_cmp_rows) * NSA_CMP_STRIDE
    sel_start = jnp.arange(n_sel) * NSA_SEL_BLOCK
    overlap = jnp.clip(jnp.minimum(cmp_start[None, :] + NSA_CMP_BLOCK, sel_start[:, None] + NSA_SEL_BLOCK)
                       - jnp.maximum(cmp_start[None, :], sel_start[:, None]), 0).astype(F32) / NSA_CMP_STRIDE
    return jnp.where(cmp_start[None, :] + NSA_CMP_BLOCK <= S, overlap, 0.0).astype(BF16)


def _nsa_attention(q_t, kc, vc_t, ks, vs_t, kw, vw_t, gates_t, bias_tiles):
    B, _, S = q_t.shape
    t = ATT_TILE
    nq = S // t
    G, R, d = NSA_KV_GROUPS, NSA_GROUP_SIZE, HEAD_DIM
    n_cmp_rows = kc.shape[2]
    overlap_t = _nsa_overlap_t(S)
    n_sel = overlap_t.shape[0]
    k_spec = pl.BlockSpec((1, 1, S, d), lambda g, b, i: (b, g, 0, 0))
    vt_spec = pl.BlockSpec((1, d, S), lambda g, b, i: (b, g, 0))
    return pl.pallas_call(
        _nsa_kernel,
        grid=(G, B, nq),
        in_specs=[pl.BlockSpec((1, R * d, t), lambda g, b, i: (b, g, i)),
                  pl.BlockSpec((1, 1, n_cmp_rows, d), lambda g, b, i: (b, g, 0, 0)),
                  pl.BlockSpec((1, 1, d, n_cmp_rows), lambda g, b, i: (b, g, 0, 0)),
                  k_spec, vt_spec, k_spec, vt_spec,
                  pl.BlockSpec((1, 1, 3 * R, t), lambda g, b, i: (b, g, 0, i)),
                  pl.BlockSpec((R, nq, t, t), lambda g, b, i: (g, 0, 0, 0)),
                  pl.BlockSpec((n_sel, n_cmp_rows), lambda g, b, i: (0, 0))],
        out_specs=pl.BlockSpec((1, R * d, t), lambda g, b, i: (b, g, i)),
        out_shape=jax.ShapeDtypeStruct((B, G * R * d, S), BF16),
        scratch_shapes=[pltpu.VMEM((n_sel, t), F32),
                        pltpu.VMEM((R, d, t), F32),
                        pltpu.VMEM((R, d, t), F32),
                        pltpu.VMEM((R, d, t), F32)],
        compiler_params=_cparams(("parallel", "parallel", "arbitrary")),
        name="nsa_attention",
    )(q_t, kc, vc_t, ks, vs_t, kw, vw_t, gates_t, bias_tiles, overlap_t)


def _qkv_head_gain(q_gain, k_gain, n_heads, n_plain):
    return jnp.concatenate([jnp.tile(q_gain, n_heads) * Q_SCALE, jnp.tile(k_gain, n_heads),
                            jnp.ones((n_plain,), F32)])


def _swap_last(x):
    return x.transpose(0, 2, 1)


def _diff_layer(h2, B, S, norm_mix, p, bias_tiles, lambda_init):
    w_in, w_out, q_gain, k_gain, lq1, lk1, lq2, lk2, subln = p
    hg = _qkv_head_gain(q_gain, k_gain, 2 * DIFF_HEADS, D_MODEL)
    qkv = _norm_proj(h2, norm_mix, w_in.astype(BF16), hg, 2 * D_MODEL).reshape(B, S, -1)
    o_t = _diff_attention(_swap_last(qkv[:, :, :D_MODEL]), qkv, _swap_last(qkv[:, :, 2 * D_MODEL:]),
                          jnp.stack([lq1, lk1, lq2, lk2]), subln, bias_tiles, lambda_init)
    return _proj_residual(_swap_last(o_t).reshape(B * S, -1), w_out.astype(BF16), h2)


def _moba_layer(h2, B, S, norm_mix, p, bias_tiles):
    w_in, w_out, q_gain, k_gain = p
    hg = _qkv_head_gain(q_gain, k_gain, N_SCORE_HEADS, D_MODEL)
    qkv = _norm_proj(h2, norm_mix, w_in.astype(BF16), hg, 2 * D_MODEL).reshape(B, S, -1)
    o_t = _moba_attention(_swap_last(qkv[:, :, :D_MODEL]), qkv, _swap_last(qkv[:, :, 2 * D_MODEL:]), bias_tiles)
    return _proj_residual(_swap_last(o_t).reshape(B * S, -1), w_out.astype(BF16), h2)


def _nsa_layer(h2, B, S, norm_mix, p, bias_tiles):
    w_in, w_out, q_gain, k_gain, pos_k, pos_v, k_w1, k_w2, v_w1, v_w2 = p
    G, d = NSA_KV_GROUPS, HEAD_DIM
    kvw = G * d
    qw = N_SCORE_HEADS * d
    sec = {name: slice(qw + i * kvw, qw + (i + 1) * kvw) for i, name in
           enumerate(("kc", "vc", "ks", "vs", "kw", "vw"))}
    n_gate = 3 * N_SCORE_HEADS
    tn = 384
    n_used = qw + 6 * kvw + n_gate
    n_pad = -n_used % tn
    w_perm = jnp.concatenate([w_in[:, :qw], w_in[:, sec["ks"]], w_in[:, sec["kw"]], w_in[:, sec["kc"]],
                              w_in[:, sec["vc"]], w_in[:, sec["vs"]], w_in[:, sec["vw"]],
                              w_in[:, qw + 6 * kvw:], jnp.zeros((D_MODEL, n_pad), F32)], axis=1)
    hg = jnp.concatenate([jnp.tile(q_gain, N_SCORE_HEADS) * Q_SCALE, jnp.tile(k_gain[1], G),
                          jnp.tile(k_gain[2], G), jnp.ones((n_used + n_pad - qw - 2 * kvw,), F32)])
    proj = _norm_proj(h2, norm_mix, w_perm.astype(BF16), hg, qw + 2 * kvw, tn=tn).reshape(B, S, -1)

    def section(i):
        return proj[:, :, qw + i * kvw:qw + (i + 1) * kvw]

    def group_major(x):
        return x.reshape(B, S, G, d).transpose(0, 2, 1, 3)

    ks, kw, kc, vc, vs, vw = (section(i) for i in range(6))
    gates_t = _swap_last(proj[:, :, qw + 6 * kvw:n_used]).reshape(B, G, n_gate // G, S)
    k_cmp = _compress(group_major(kc), pos_k, k_w1.astype(BF16), k_w2.astype(BF16), k_gain[0], True)
    v_cmp = _compress(group_major(vc), pos_v, v_w1.astype(BF16), v_w2.astype(BF16), k_gain[0], False)
    o_t = _nsa_attention(_swap_last(proj[:, :, :qw]), k_cmp, v_cmp.transpose(0, 1, 3, 2),
                         group_major(ks), _swap_last(vs), group_major(kw), _swap_last(vw), gates_t, bias_tiles)
    return _proj_residual(_swap_last(o_t).reshape(B * S, qw), w_out.astype(BF16), h2)


def _diff_lambda_init(layer):
    return 0.8 - 0.6 * math.exp(-0.3 * layer)


def kernel(x, rel_bias,
           l0_norm_mix, l0_w_in, l0_w_out, l0_q_gain, l0_k_gain, l0_lam_q1, l0_lam_k1, l0_lam_q2, l0_lam_k2,
           l0_subln_gain, l0_norm_ffn, l0_w_gate, l0_w_up, l0_w_down,
           l1_norm_mix, l1_w_in, l1_w_out, l1_q_gain, l1_k_gain, l1_norm_ffn, l1_w_router, l1_b_router,
           l1_we_gate, l1_we_up, l1_we_down,
           l2_norm_mix, l2_w_in, l2_w_out, l2_q_gain, l2_k_gain, l2_cmp_pos_k, l2_cmp_pos_v, l2_cmp_k_w1,
           l2_cmp_k_w2, l2_cmp_v_w1, l2_cmp_v_w2, l2_norm_ffn, l2_w_gate, l2_w_up, l2_w_down,
           l3_norm_mix, l3_w_in, l3_w_out, l3_q_gain, l3_k_gain, l3_lam_q1, l3_lam_k1, l3_lam_q2, l3_lam_k2,
           l3_subln_gain, l3_norm_ffn, l3_w_router, l3_b_router, l3_we_gate, l3_we_up, l3_we_down):
    B, S, D = x.shape
    bias_tiles = _bias_tiles(rel_bias, S, ATT_TILE)
    h = x.reshape(B * S, D)

    h = _diff_layer(h, B, S, l0_norm_mix, (l0_w_in, l0_w_out, l0_q_gain, l0_k_gain, l0_lam_q1, l0_lam_k1,
                                           l0_lam_q2, l0_lam_k2, l0_subln_gain), bias_tiles, _diff_lambda_init(0))
    h = _dense_ffn(h, l0_norm_ffn, l0_w_gate.astype(BF16), l0_w_up.astype(BF16), l0_w_down.astype(BF16))

    h = _moba_layer(h, B, S, l1_norm_mix, (l1_w_in, l1_w_out, l1_q_gain, l1_k_gain), bias_tiles)
    h = _moe_ffn(h, l1_norm_ffn, l1_w_router, l1_b_router, l1_we_gate.astype(BF16), l1_we_up.astype(BF16),
                 l1_we_down.astype(BF16))

    h = _nsa_layer(h, B, S, l2_norm_mix, (l2_w_in, l2_w_out, l2_q_gain, l2_k_gain, l2_cmp_pos_k, l2_cmp_pos_v,
                                          l2_cmp_k_w1, l2_cmp_k_w2, l2_cmp_v_w1, l2_cmp_v_w2), bias_tiles)
    h = _dense_ffn(h, l2_norm_ffn, l2_w_gate.astype(BF16), l2_w_up.astype(BF16), l2_w_down.astype(BF16))

    h = _diff_layer(h, B, S, l3_norm_mix, (l3_w_in, l3_w_out, l3_q_gain, l3_k_gain, l3_lam_q1, l3_lam_k1,
                                           l3_lam_q2, l3_lam_k2, l3_subln_gain), bias_tiles, _diff_lambda_init(3))
    h = _moe_ffn(h, l3_norm_ffn, l3_w_router, l3_b_router, l3_we_gate.astype(BF16), l3_we_up.astype(BF16),
                 l3_we_down.astype(BF16))
    return h.reshape(B, S, D)
```
